```python
import jax, jax.numpy as jnp
from jax import lax
import numpy as np

D_MODEL = 1024
BATCH = 8
SEQ = 4096
DEPTH = 4
DEC_BATCH = 32
DEC_SEQ = 16
PAST_LEN = 2048

CHUNK = 64
Q_BLOCK = 128
DSA_Q_BLOCK = 32
ROPE_THETA = 500000.0
LN_EPS = 1e-5
RMS_EPS = 1e-6
DN_ALPHA = (2 * DEPTH) ** 0.25
DN_BETA = (8 * DEPTH) ** -0.25

H_A = 8
Q_LORA = 384
KV_LORA = 256
NOPE_A = 64
ROPE_A = 32
V_A = 64
H_B = 8
HD_B = 64
FORGET_BIAS = 2.0
H_C = 8
HD_C = 64
ROT_C = HD_C // 4
IDX_H = 8
IDX_DIM = 32
ROT_IDX = IDX_DIM // 4
TOPK_MAX = 256
N_BRANCH = 3
BRANCH_W = H_A * V_A
N_EXPERTS = 32
TOP_K = 4
D_FF = 1024
SWIGLU_LIMIT = 7.0
SWIGLU_ALPHA = 1.702
MOE_BLOCK = 128
N_STATE = 8

IN_SPLITS = (Q_LORA, KV_LORA, ROPE_A,
             H_B * HD_B, H_B * HD_B, H_B * HD_B, H_B,
             H_C * HD_C, H_C * HD_C, H_C * HD_C, IDX_H * IDX_DIM, IDX_DIM, IDX_H,
             N_BRANCH * D_MODEL)
IN_WIDTH = sum(IN_SPLITS)

kernel_name = 'hybrid_stream_mla_fox_dsa_moe_step'


def split_points():
    pts, acc = [], 0
    for w in IN_SPLITS[:-1]:
        acc += w
        pts.append(acc)
    return pts


def layer_norm(x, g, b):
    xf = x.astype(jnp.float32)
    xc = xf - jnp.mean(xf, -1, keepdims=True)
    var = jnp.mean(xc * xc, -1, keepdims=True)
    return (xc * lax.rsqrt(var + LN_EPS) * g + b).astype(x.dtype)


def rms_norm(x, g):
    xf = x.astype(jnp.float32)
    return (xf * lax.rsqrt(jnp.mean(xf * xf, -1, keepdims=True) + RMS_EPS) * g).astype(x.dtype)


def rope(x, pos, rot):
    half = rot // 2
    inv_freq = ROPE_THETA ** (-jnp.arange(half, dtype=jnp.float32) / half)
    ang = pos.astype(jnp.float32)[:, None] * inv_freq
    bshape = (pos.shape[0],) + (1,) * (x.ndim - 3) + (half,)
    cos = jnp.cos(ang).reshape(bshape)
    sin = jnp.sin(ang).reshape(bshape)
    xf = x.astype(jnp.float32)
    x1 = xf[..., :half]
    x2 = xf[..., half:rot]
    return jnp.concatenate([x1 * cos - x2 * sin, x2 * cos + x1 * sin, xf[..., rot:]], -1).astype(x.dtype)


def chunk_mask(q_pos, k_pos):
    return (k_pos[None, :] // CHUNK) <= (q_pos[:, None] // CHUNK)


def over_query_blocks(fn, n_q, block):
    if n_q <= block or n_q % block:
        return fn(0, n_q)
    out = lax.map(lambda i: fn(i * block, block), jnp.arange(n_q // block))
    out = jnp.moveaxis(out, 0, 1)
    return out.reshape((out.shape[0], n_q) + out.shape[3:])


def mla_attention(q_nope, q_rope, k_nope, k_rope, v, q_pos, k_pos):
    scale = (NOPE_A + ROPE_A) ** -0.5

    def block(st, n):
        qn = lax.dynamic_slice_in_dim(q_nope, st, n, 1)
        qr = lax.dynamic_slice_in_dim(q_rope, st, n, 1)
        qp = lax.dynamic_slice_in_dim(q_pos, st, n, 0)
        s = (jnp.einsum('bqhd,bkhd->bhqk', qn, k_nope, preferred_element_type=jnp.float32)
             + jnp.einsum('bqhr,bkr->bhqk', qr, k_rope, preferred_element_type=jnp.float32)) * scale
        s = jnp.where(chunk_mask(qp, k_pos), s, -jnp.inf)
        p = jax.nn.softmax(s, axis=-1).astype(v.dtype)
        return jnp.einsum('bhqk,bkhd->bqhd', p, v)

    o = over_query_blocks(block, q_nope.shape[1], Q_BLOCK)
    return o.reshape(o.shape[0], o.shape[1], H_A * V_A)


def fox_attention(q, k, v, cum_q, cum_k, q_pos, k_pos):
    scale = HD_B ** -0.5
    ck = jnp.swapaxes(cum_k, 1, 2)[:, :, None, :]

    def block(st, n):
        qb = lax.dynamic_slice_in_dim(q, st, n, 1)
        cq = lax.dynamic_slice_in_dim(cum_q, st, n, 1)
        qp = lax.dynamic_slice_in_dim(q_pos, st, n, 0)
        s = jnp.einsum('bqhd,bkhd->bhqk', qb, k, preferred_element_type=jnp.float32) * scale
        s = s + jnp.swapaxes(cq, 1, 2)[..., None] - ck
        s = jnp.where(k_pos[None, :] <= qp[:, None], s, -jnp.inf)
        p = jax.nn.softmax(s, axis=-1).astype(v.dtype)
        return jnp.einsum('bhqk,bkhd->bqhd', p, v)

    o = over_query_blocks(block, q.shape[1], Q_BLOCK)
    return o.reshape(o.shape[0], o.shape[1], H_B * HD_B)


def dsa_attention(q, k, v, iq, ik, iw, q_pos, k_pos):
    n_keep = min(TOPK_MAX, k.shape[1] // 4)
    scale = HD_C ** -0.5
    iscale = IDX_DIM ** -0.5
    wscale = IDX_H ** -0.5

    def block(st, n):
        qb = lax.dynamic_slice_in_dim(q, st, n, 1)
        iqb = lax.dynamic_slice_in_dim(iq, st, n, 1)
        iwb = lax.dynamic_slice_in_dim(iw, st, n, 1).astype(jnp.float32) * wscale
        qp = lax.dynamic_slice_in_dim(q_pos, st, n, 0)
        rel = jax.nn.relu(jnp.einsum('bqhd,bkd->bqhk', iqb, ik, preferred_element_type=jnp.float32) * iscale)
        score = jnp.einsum('bqh,bqhk->bqk', iwb, rel)
        score = jnp.where(chunk_mask(qp, k_pos)[None], score, -jnp.inf)
        top_s, top_i = lax.top_k(score, n_keep)
        valid = jnp.isfinite(top_s)
        ksel = jax.vmap(lambda kb, ib: kb[ib])(k, top_i)
        vsel = jax.vmap(lambda vb, ib: vb[ib])(v, top_i)
        s = jnp.einsum('bqhd,bqshd->bhqs', qb, ksel, preferred_element_type=jnp.float32) * scale
        s = jnp.where(valid[:, None], s, -jnp.inf)
        p = jax.nn.softmax(s, axis=-1).astype(v.dtype)
        return jnp.einsum('bhqs,bqshd->bqhd', p, vsel)

    o = over_query_blocks(block, q.shape[1], DSA_Q_BLOCK)
    return o.reshape(o.shape[0], o.shape[1], H_C * HD_C)


def token_mixers(h, cache, w_in, b_f, b_gate, g_qa, g_kva, w_uq, w_ukv, w_br, w_out):
    B, T, _ = h.shape
    past = 0 if cache is None else cache[0].shape[1]
    Tk = past + T
    k_pos = jnp.arange(Tk, dtype=jnp.int32)
    pos = k_pos[past:]
    if cache is None:
        cache = (None,) * N_STATE
    c_ckv, c_kr, c_kb, c_vb, c_lf, c_kc, c_vc, c_ik = cache

    def with_past(old, new):
        return new if old is None else jnp.concatenate([old, new], axis=1)

    z = h @ w_in
    qa, kva, kra, qb, kb, vb, fb, qc, kc, vc, iqc, ikc, iwc, gt = jnp.split(z, split_points(), axis=-1)

    q_a = (rms_norm(qa, g_qa) @ w_uq).reshape(B, T, H_A, NOPE_A + ROPE_A)
    qa_nope = q_a[..., :NOPE_A]
    qa_rope = rope(q_a[..., NOPE_A:], pos, ROPE_A)
    ckv_new = rms_norm(kva, g_kva)
    kr_new = rope(kra, pos, ROPE_A)
    kr_all = with_past(c_kr, kr_new)
    kv = (with_past(c_ckv, ckv_new) @ w_ukv).reshape(B, Tk, H_A, NOPE_A + V_A)
    o_a = mla_attention(qa_nope, qa_rope, kv[..., :NOPE_A], kr_all, kv[..., NOPE_A:], pos, k_pos)

    kb_new = kb.reshape(B, T, H_B, HD_B)
    vb_new = vb.reshape(B, T, H_B, HD_B)
    lf_new = jax.nn.log_sigmoid(fb.astype(jnp.float32) + b_f.astype(jnp.float32))
    cum = jnp.cumsum(with_past(c_lf, lf_new).astype(jnp.float32), axis=1)
    o_b = fox_attention(qb.reshape(B, T, H_B, HD_B), with_past(c_kb, kb_new), with_past(c_vb, vb_new),
                        cum[:, past:], cum, pos, k_pos)

    qc_r = rope(qc.reshape(B, T, H_C, HD_C), pos, ROT_C)
    kc_new = rope(kc.reshape(B, T, H_C, HD_C), pos, ROT_C)
    vc_new = vc.reshape(B, T, H_C, HD_C)
    iq = rope(iqc.reshape(B, T, IDX_H, IDX_DIM), pos, ROT_IDX)
    ik_new = rope(ikc, pos, ROT_IDX)
    o_c = dsa_attention(qc_r, with_past(c_kc, kc_new), with_past(c_vc, vc_new), iq,
                        with_past(c_ik, ik_new), iwc, pos, k_pos)

    branches = jnp.stack([o_a, o_b, o_c], axis=2)
    proj = jnp.einsum('btnc,ncd->btnd', branches, w_br)
    gates = jax.nn.sigmoid((gt.reshape(B, T, N_BRANCH, D_MODEL) + b_gate).astype(jnp.float32))
    merged = jnp.sum(gates * proj, axis=2).astype(h.dtype)
    out = merged @ w_out
    return out, (ckv_new, kr_new, kb_new, vb_new, lf_new, kc_new, vc_new, ik_new)


def moe(h, w_r, b_r, w1, b1, w2, b2):
    B, T, D = h.shape
    x = h.reshape(-1, D)
    n = x.shape[0]
    logits = (x @ w_r + b_r).astype(jnp.float32)
    top_val, top_idx = lax.top_k(logits, TOP_K)
    gate = jax.nn.softmax(top_val, axis=-1)
    flat_e = top_idx.reshape(-1)
    flat_tok = jnp.arange(n * TOP_K, dtype=jnp.int32) // TOP_K
    flat_g = gate.reshape(-1)
    order = jnp.argsort(flat_e)
    se = flat_e[order]
    counts = jnp.bincount(flat_e, length=N_EXPERTS)
    padded = (counts + MOE_BLOCK - 1) // MOE_BLOCK * MOE_BLOCK
    start = jnp.cumsum(counts) - counts
    pend = jnp.cumsum(padded)
    pstart = pend - padded
    dest = pstart[se] + jnp.arange(n * TOP_K, dtype=jnp.int32) - start[se]
    n_blocks = -(-(n * TOP_K) // MOE_BLOCK) + N_EXPERTS
    cap = n_blocks * MOE_BLOCK
    slot_tok = jnp.full((cap,), n, jnp.int32).at[dest].set(flat_tok[order])
    slot_g = jnp.zeros((cap,), jnp.float32).at[dest].set(flat_g[order])
    blk_e = jnp.minimum(jnp.searchsorted(pend, jnp.arange(n_blocks) * MOE_BLOCK, side='right'), N_EXPERTS - 1)
    xpad = jnp.concatenate([x, jnp.zeros((1, D), x.dtype)], axis=0)
    xb = xpad[slot_tok].reshape(n_blocks, MOE_BLOCK, D)

    def expert_block(args):
        xe, e = args
        gu = xe @ w1[e] + b1[e]
        g_ = jnp.minimum(gu[:, 0::2], SWIGLU_LIMIT)
        u_ = jnp.clip(gu[:, 1::2], -SWIGLU_LIMIT, SWIGLU_LIMIT)
        a = g_ * jax.nn.sigmoid(SWIGLU_ALPHA * g_) * (u_ + 1.0)
        return a @ w2[e] + b2[e]

    yb = lax.map(expert_block, (xb, blk_e)).reshape(cap, D)
    y = jnp.zeros((n + 1, D), jnp.float32).at[slot_tok].add(yb.astype(jnp.float32) * slot_g[:, None])[:n]
    return y.astype(h.dtype).reshape(B, T, D)


def trunk(x, cache, ln_in_g, ln_in_b, w_in, b_f, b_gate, g_qa, g_kva, w_uq, w_ukv, w_br, w_out,
          ln1_g, ln1_b, w_router, b_router, w_up, b_up, w_down, b_down, ln2_g, ln2_b):
    x = layer_norm(x, ln_in_g, ln_in_b)
    new = [[] for _ in range(N_STATE)]
    for l in range(DEPTH):
        cache_l = None if cache is None else tuple(c[l] for c in cache)
        mix, rows = token_mixers(x, cache_l, w_in[l], b_f[l], b_gate[l], g_qa[l], g_kva[l],
                                 w_uq[l], w_ukv[l], w_br[l], w_out[l])
        x = layer_norm(DN_ALPHA * x + mix, ln1_g[l], ln1_b[l])
        ffn = moe(x, w_router[l], b_router[l], w_up[l], b_up[l], w_down[l], b_down[l])
        x = layer_norm(DN_ALPHA * x + ffn, ln2_g[l], ln2_b[l])
        for acc, r in zip(new, rows):
            acc.append(r)
    return x, [jnp.stack(a) for a in new]


def setup_inputs(seed: int = 0) -> dict:
    key = jax.random.key(seed)
    keys = jax.random.split(key, 40)
    counter = [0]

    def nrm(shape, scale=1.0):
        k = keys[counter[0]]
        counter[0] += 1
        return jax.random.normal(k, shape, jnp.float32) * scale

    L, E = DEPTH, N_EXPERTS
    cb = (L, DEC_BATCH, PAST_LEN)
    return {
        'x_prompt': nrm((BATCH, SEQ, D_MODEL)),
        'x_sample': nrm((DEC_BATCH, DEC_SEQ, D_MODEL)),
        'cache_mla_ckv': nrm(cb + (KV_LORA,)),
        'cache_mla_krope': nrm(cb + (ROPE_A,)),
        'cache_fox_k': nrm(cb + (H_B, HD_B)),
        'cache_fox_v': nrm(cb + (H_B, HD_B)),
        'cache_fox_logf': jax.nn.log_sigmoid(nrm(cb + (H_B,)) + FORGET_BIAS),
        'cache_dsa_k': nrm(cb + (H_C, HD_C)),
        'cache_dsa_v': nrm(cb + (H_C, HD_C)),
        'cache_dsa_idxk': nrm(cb + (IDX_DIM,)),
        'ln_in_g': 1.0 + nrm((D_MODEL,), 0.02),
        'ln_in_b': nrm((D_MODEL,), 0.02),
        'w_in': nrm((L, D_MODEL, IN_WIDTH), D_MODEL ** -0.5),
        'b_f': FORGET_BIAS + nrm((L, H_B), 0.1),
        'b_gate': nrm((L, N_BRANCH, D_MODEL), 0.1),
        'g_qa': 1.0 + nrm((L, Q_LORA), 0.02),
        'g_kva': 1.0 + nrm((L, KV_LORA), 0.02),
        'w_uq': nrm((L, Q_LORA, H_A * (NOPE_A + ROPE_A)), Q_LORA ** -0.5),
        'w_ukv': nrm((L, KV_LORA, H_A * (NOPE_A + V_A)), KV_LORA ** -0.5),
        'w_br': nrm((L, N_BRANCH, BRANCH_W, D_MODEL), BRANCH_W ** -0.5),
        'w_out': nrm((L, D_MODEL, D_MODEL), D_MODEL ** -0.5 * DN_BETA),
        'ln1_g': 1.0 + nrm((L, D_MODEL), 0.02),
        'ln1_b': nrm((L, D_MODEL), 0.02),
        'w_router': nrm((L, D_MODEL, E), D_MODEL ** -0.5),
        'b_router': nrm((L, E), 0.01),
        'w_up': nrm((L, E, D_MODEL, 2 * D_FF), D_MODEL ** -0.5),
        'b_up': nrm((L, E, 2 * D_FF), 0.01),
        'w_down': nrm((L, E, D_FF, D_MODEL), D_FF ** -0.5 * DN_BETA),
        'b_down': nrm((L, E, D_MODEL), 0.01),
        'ln2_g': 1.0 + nrm((L, D_MODEL), 0.02),
        'ln2_b': nrm((L, D_MODEL), 0.02),
    }


def reference(x_prompt, x_sample, cache_mla_ckv, cache_mla_krope, cache_fox_k, cache_fox_v, cache_fox_logf,
              cache_dsa_k, cache_dsa_v, cache_dsa_idxk, ln_in_g, ln_in_b, w_in, b_f, b_gate, g_qa, g_kva,
              w_uq, w_ukv, w_br, w_out, ln1_g, ln1_b, w_router, b_router, w_up, b_up, w_down, b_down,
              ln2_g, ln2_b):
    weights = (ln_in_g, ln_in_b, w_in, b_f, b_gate, g_qa, g_kva, w_uq, w_ukv, w_br, w_out,
               ln1_g, ln1_b, w_router, b_router, w_up, b_up, w_down, b_down, ln2_g, ln2_b)
    y_prompt, st_p = trunk(x_prompt, None, *weights)
    caches = (cache_mla_ckv, cache_mla_krope, cache_fox_k, cache_fox_v, cache_fox_logf,
              cache_dsa_k, cache_dsa_v, cache_dsa_idxk)
    y_sample, st_s = trunk(x_sample, caches, *weights)
    ckv_p, kr_p, fk_p, fv_p, lf_p, ck_p, cv_p, ik_p = st_p
    ckv_s, kr_s, fk_s, fv_s, lf_s, ck_s, cv_s, ik_s = st_s
    return (y_prompt, y_sample, ckv_p, ckv_s, kr_p, kr_s, fk_p, fk_s, fv_p, fv_s, lf_p, lf_s,
            ck_p, ck_s, cv_p, cv_s, ik_p, ik_s)
```

```python
import functools

import jax
import jax.numpy as jnp
from jax import lax
from jax.experimental import pallas as pl
from jax.experimental.pallas import tpu as pltpu

F32 = jnp.float32
BF16 = jnp.bfloat16
I32 = jnp.int32

D_MODEL = 1024
CHUNK = 64
CHUNK_SHIFT = 6
ROPE_THETA = 500000.0
LN_EPS = 1e-5
RMS_EPS = 1e-6
DEPTH_NOMINAL = 4
DN_ALPHA = (2 * DEPTH_NOMINAL) ** 0.25
N_HEADS = 8
HEAD_DIM = 64
Q_LORA = 384
KV_LORA = 256
ROPE_A = 32
ROT_C = 16
IDX_H = 8
IDX_DIM = 32
ROT_IDX = 8
TOPK_MAX = 256
N_BRANCH = 3
BRANCH_W = N_HEADS * HEAD_DIM
N_EXPERTS = 32
TOP_K = 4
D_FF = 1024
SWIGLU_LIMIT = 7.0
SWIGLU_ALPHA = 1.702

LANES = 128
VMEM_LIMIT_BYTES = 56 * 2**20

NEG = -1e30
INT_MIN = -2**31
KEY_NEG_INF = -2139095041
BIG_IDX = 2**30


def _cparams(*sem):
    return pltpu.CompilerParams(dimension_semantics=sem, vmem_limit_bytes=VMEM_LIMIT_BYTES)


def _tile(n, pref):
    return pref if n % pref == 0 else n


def _full(shape):
    zeros = (0,) * len(shape)
    return pl.BlockSpec(shape, lambda *_: zeros)


def _dot(a, b):
    return jnp.dot(a, b, preferred_element_type=F32)


def _dot_nt(a, b):
    return lax.dot_general(a, b, (((1,), (1,)), ((), ())), preferred_element_type=F32)


def _layer_norm(x, g, b):
    xc = x - jnp.mean(x, axis=-1, keepdims=True)
    var = jnp.mean(xc * xc, axis=-1, keepdims=True)
    return xc * lax.rsqrt(var + LN_EPS) * g + b


def _rms_norm(x, g):
    return x * lax.rsqrt(jnp.mean(x * x, axis=-1, keepdims=True) + RMS_EPS) * g


def _rope_slab(x, c, sa, sb, half):
    return x * c + pltpu.roll(x, half, 1) * sa + pltpu.roll(x, LANES - half, 1) * sb


def _rope_tables(pos, rot, period):
    half = rot // 2
    inv_freq = ROPE_THETA ** (-jnp.arange(half, dtype=F32) / half)
    ang = pos.astype(F32)[:, None] * inv_freq
    j = jnp.arange(LANES) % period
    cos = jnp.cos(ang)[:, j % half]
    sin = jnp.sin(ang)[:, j % half]
    c = jnp.where(j < rot, cos, 1.0)
    sa = jnp.where((j >= half) & (j < rot), sin, 0.0)
    sb = jnp.where(j < half, -sin, 0.0)
    return c, sa, sb


def _ln_kernel(x_ref, g_ref, b_ref, o_ref, ob_ref):
    y = _layer_norm(x_ref[...], g_ref[...], b_ref[...])
    o_ref[...] = y
    ob_ref[...] = y.astype(BF16)


def _ln_in(x, g, b):
    n, d = x.shape
    tm = _tile(n, 512)
    row = pl.BlockSpec((tm, d), lambda i: (i, 0))
    return pl.pallas_call(
        _ln_kernel, grid=(n // tm,),
        in_specs=[row, _full((1, d)), _full((1, d))],
        out_specs=[row, row],
        out_shape=[jax.ShapeDtypeStruct((n, d), F32), jax.ShapeDtypeStruct((n, d), BF16)],
        compiler_params=_cparams("parallel"),
    )(x, g.reshape(1, d), b.reshape(1, d))


def _mla_proj_kernel(hb_ref, w_ref, wuq_ref, gqa_ref, gkva_ref, c_ref, sa_ref, sb_ref,
                     qn_ref, qr_ref, ckv_ref, kr_ref):
    z = _dot(hb_ref[0], w_ref[...])
    qa = _rms_norm(z[:, :Q_LORA], gqa_ref[...])
    q = _dot(qa.astype(BF16), wuq_ref[...])
    qn_ref[0] = q[:, :BRANCH_W].astype(BF16)
    c, sa, sb = c_ref[...], sa_ref[...], sb_ref[...]
    for s in range(2):
        lo = BRANCH_W + s * LANES
        qr_ref[0, :, s * LANES:(s + 1) * LANES] = _rope_slab(q[:, lo:lo + LANES], c, sa, sb, ROPE_A // 2).astype(BF16)
    ckv_ref[0] = _rms_norm(z[:, Q_LORA:Q_LORA + KV_LORA], gkva_ref[...])
    kr = _rope_slab(z[:, Q_LORA + KV_LORA:], c, sa, sb, ROPE_A // 2)
    kr_ref[0] = kr[:, :ROPE_A]


def _fox_proj_kernel(hb_ref, w_ref, bf_ref, q_ref, k_ref, v_ref, k16_ref, v16_ref, lf_ref):
    z = _dot(hb_ref[0], w_ref[...])
    w = BRANCH_W
    q_ref[0] = (z[:, :w] * HEAD_DIM ** -0.5).astype(BF16)
    k = z[:, w:2 * w]
    v = z[:, 2 * w:3 * w]
    k_ref[0] = k
    v_ref[0] = v
    k16_ref[0] = k.astype(BF16)
    v16_ref[0] = v.astype(BF16)
    x = z[:, 3 * w:3 * w + N_HEADS] + bf_ref[...]
    lf_ref[0] = jnp.minimum(x, 0.0) - jnp.log1p(jnp.exp(-jnp.abs(x)))


def _dsa_proj_kernel(hb_ref, w_ref, c64_ref, sa64_ref, sb64_ref, c32_ref, sa32_ref, sb32_ref,
                     q_ref, k_ref, v_ref, k16_ref, v16_ref, iq_ref, ik_ref, iw_ref):
    z = _dot(hb_ref[0], w_ref[...])
    w = BRANCH_W
    c64, sa64, sb64 = c64_ref[...], sa64_ref[...], sb64_ref[...]
    c32, sa32, sb32 = c32_ref[...], sa32_ref[...], sb32_ref[...]
    for s in range(w // LANES):
        sl = slice(s * LANES, (s + 1) * LANES)
        qs = _rope_slab(z[:, s * LANES:(s + 1) * LANES], c64, sa64, sb64, ROT_C // 2)
        q_ref[0, :, sl] = (qs * HEAD_DIM ** -0.5).astype(BF16)
        ks = _rope_slab(z[:, w + s * LANES:w + (s + 1) * LANES], c64, sa64, sb64, ROT_C // 2)
        k_ref[0, :, sl] = ks
        k16_ref[0, :, sl] = ks.astype(BF16)
    v = z[:, 2 * w:3 * w]
    v_ref[0] = v
    v16_ref[0] = v.astype(BF16)
    o = 3 * w
    for s in range(2):
        iq_ref[0, :, s * LANES:(s + 1) * LANES] = _rope_slab(
            z[:, o + s * LANES:o + (s + 1) * LANES], c32, sa32, sb32, ROT_IDX // 2).astype(BF16)
    o += IDX_H * IDX_DIM
    ik = _rope_slab(z[:, o:o + LANES], c32, sa32, sb32, ROT_IDX // 2)
    ik_ref[0] = ik[:, :IDX_DIM]
    o += LANES
    iw_ref[0] = z[:, o:o + IDX_H] * (IDX_DIM ** -0.5 * IDX_H ** -0.5)


def _proj_call(kernel, hb, weights, tables, out_widths_dtypes, tm):
    B, T, D = hb.shape
    grid = (B, T // tm)
    in_specs = [pl.BlockSpec((1, tm, D), lambda b, t: (b, t, 0))]
    in_specs += [_full(w.shape) for w in weights]
    in_specs += [pl.BlockSpec((tm, LANES), lambda b, t: (t, 0)) for _ in tables]
    out_specs = [pl.BlockSpec((1, tm, w), lambda b, t: (b, t, 0)) for w, _ in out_widths_dtypes]
    out_shape = [jax.ShapeDtypeStruct((B, T, w), dt) for w, dt in out_widths_dtypes]
    return pl.pallas_call(kernel, grid=grid, in_specs=in_specs, out_specs=out_specs, out_shape=out_shape,
                          compiler_params=_cparams("parallel", "parallel"))(hb, *weights, *tables)


def _mla_kv_kernel(ckv_ref, kr_ref, w_ref, tile_ref, kpp_ref, v_ref):
    kv = _dot(ckv_ref[0].astype(BF16), w_ref[...])
    kr4 = _dot(kr_ref[0].astype(BF16), tile_ref[...]).astype(BF16)
    for p in range(N_HEADS // 2):
        kpp_ref[0, p, :, :LANES] = kv[:, p * LANES:(p + 1) * LANES].astype(BF16)
        kpp_ref[0, p, :, LANES:] = kr4
    v_ref[0] = kv[:, BRANCH_W:].astype(BF16)


def _mla_kv(ckv, kr, w_ukv_p, tile_mat):
    B, Tk, _ = ckv.shape
    tm = _tile(Tk, 512)
    return pl.pallas_call(
        _mla_kv_kernel, grid=(B, Tk // tm),
        in_specs=[pl.BlockSpec((1, tm, KV_LORA), lambda b, t: (b, t, 0)),
                  pl.BlockSpec((1, tm, ROPE_A), lambda b, t: (b, t, 0)),
                  _full(w_ukv_p.shape), _full(tile_mat.shape)],
        out_specs=[pl.BlockSpec((1, N_HEADS // 2, tm, 2 * LANES), lambda b, t: (b, 0, t, 0)),
                   pl.BlockSpec((1, tm, BRANCH_W), lambda b, t: (b, t, 0))],
        out_shape=[jax.ShapeDtypeStruct((B, N_HEADS // 2, Tk, 2 * LANES), BF16),
                   jax.ShapeDtypeStruct((B, Tk, BRANCH_W), BF16)],
        compiler_params=_cparams("parallel", "parallel"),
    )(ckv, kr, w_ukv_p, tile_mat)


def _cumsum_kernel(x_ref, u_ref, o_ref, carry_ref):
    @pl.when(pl.program_id(1) == 0)
    def _():
        carry_ref[...] = jnp.zeros_like(carry_ref)

    x = x_ref[0]
    u = u_ref[...]
    hi = x.astype(BF16)
    r1 = x - hi.astype(F32)
    mid = r1.astype(BF16)
    lo = (r1 - mid.astype(F32)).astype(BF16)
    out = _dot(hi, u) + _dot(mid, u) + _dot(lo, u) + carry_ref[...]
    o_ref[0] = out
    tc = x.shape[1]
    carry_ref[...] = out[:, tc - 1:tc]


def _cumsum_time(x):
    B, H, Tk = x.shape
    tc = _tile(Tk, 512)
    u = (jnp.arange(tc)[:, None] <= jnp.arange(tc)[None, :]).astype(BF16)
    return pl.pallas_call(
        _cumsum_kernel, grid=(B, Tk // tc),
        in_specs=[pl.BlockSpec((1, H, tc), lambda b, t: (b, 0, t)), _full((tc, tc))],
        out_specs=pl.BlockSpec((1, H, tc), lambda b, t: (b, 0, t)),
        out_shape=jax.ShapeDtypeStruct((B, H, Tk), F32),
        scratch_shapes=[pltpu.VMEM((H, 1), F32)],
        compiler_params=_cparams("parallel", "arbitrary"),
    )(x, u)


def _own_lanes(lane, h, width):
    per = LANES // width
    return jnp.right_shift(lane, width.bit_length() - 1) == (h % per)


def _keep_head(x, lane, h, width):
    return jnp.where(_own_lanes(lane, h, width), x.astype(F32), 0.0).astype(BF16)


def _softmax_step(h, s, v_slab, m_ref, l_ref, acc_ref):
    m_prev = m_ref[h]
    m_new = jnp.maximum(m_prev, jnp.max(s, axis=1, keepdims=True))
    alpha = jnp.exp(m_prev - m_new)
    p = jnp.exp(s - m_new)
    l_ref[h] = alpha * l_ref[h] + jnp.sum(p, axis=1, keepdims=True)
    acc_ref[h] = alpha * acc_ref[h] + _dot(p.astype(BF16), v_slab)
    m_ref[h] = m_new


def _init_softmax(m_ref, l_ref, acc_ref):
    m_ref[...] = jnp.full(m_ref.shape, NEG, F32)
    l_ref[...] = jnp.zeros(l_ref.shape, F32)
    acc_ref[...] = jnp.zeros(acc_ref.shape, F32)


def _write_heads(o_ref, l_ref, acc_ref, lane):
    for p in range(N_HEADS // 2):
        a = acc_ref[2 * p] * (1.0 / l_ref[2 * p])
        b = acc_ref[2 * p + 1] * (1.0 / l_ref[2 * p + 1])
        o_ref[0, :, p * LANES:(p + 1) * LANES] = jnp.where(lane < HEAD_DIM, a, b).astype(BF16)


def _softmax_scratch(tq):
    return [pltpu.VMEM((N_HEADS, tq, 1), F32), pltpu.VMEM((N_HEADS, tq, 1), F32),
            pltpu.VMEM((N_HEADS, tq, LANES), F32)]


def _mla_attn_kernel(qn_ref, qr_ref, kpp_ref, v_ref, o_ref, qs_ref, m_ref, l_ref, acc_ref,
                     *, tq, tk, past, n_keys, padded):
    qi = pl.program_id(1)
    ki = pl.program_id(2)
    q0 = past + qi * tq
    last_k = (jnp.minimum(n_keys, ((q0 + tq - 1) // CHUNK + 1) * CHUNK) - 1) // tk
    lane = lax.broadcasted_iota(I32, (tq, LANES), 1)

    @pl.when(ki == 0)
    def _init():
        _init_softmax(m_ref, l_ref, acc_ref)
        for h in range(N_HEADS):
            qn = qn_ref[0, :, (h // 2) * LANES:(h // 2 + 1) * LANES]
            qr = qr_ref[0, :, (h // 4) * LANES:(h // 4 + 1) * LANES]
            qs_ref[h, :, :LANES] = _keep_head(qn, lane, h, HEAD_DIM)
            qs_ref[h, :, LANES:] = _keep_head(qr, lane, h, ROPE_A)

    @pl.when(ki <= last_k)
    def _step():
        qpos = q0 + lax.broadcasted_iota(I32, (tq, tk), 0)
        kpos = ki * tk + lax.broadcasted_iota(I32, (tq, tk), 1)
        kchunk = jnp.right_shift(kpos, CHUNK_SHIFT)
        if padded:
            kchunk = jnp.where(kpos < n_keys, kchunk, BIG_IDX)
        ok = kchunk <= jnp.right_shift(qpos, CHUNK_SHIFT)
        scale = (HEAD_DIM + ROPE_A) ** -0.5
        for h in range(N_HEADS):
            p = h // 2
            s = _dot_nt(qs_ref[h], kpp_ref[0, p]) * scale
            s = jnp.where(ok, s, NEG)
            _softmax_step(h, s, v_ref[0, :, p * LANES:(p + 1) * LANES], m_ref, l_ref, acc_ref)

    @pl.when(ki == pl.num_programs(2) - 1)
    def _fin():
        _write_heads(o_ref, l_ref, acc_ref, lane)


def _mla_attention(qn, qr, kpp, v, past, n_keys):
    B, T, _ = qn.shape
    Tkp = v.shape[1]
    tq = _tile(T, 512)
    tk = _tile(Tkp, 512)

    def kblk(qi, ki):
        last = (jnp.minimum(n_keys, ((past + qi * tq + tq - 1) // CHUNK + 1) * CHUNK) - 1) // tk
        return jnp.minimum(ki, last)

    kern = functools.partial(_mla_attn_kernel, tq=tq, tk=tk, past=past, n_keys=n_keys, padded=Tkp != n_keys)
    return pl.pallas_call(
        kern, grid=(B, T // tq, Tkp // tk),
        in_specs=[pl.BlockSpec((1, tq, BRANCH_W), lambda b, qi, ki: (b, qi, 0)),
                  pl.BlockSpec((1, tq, 2 * LANES), lambda b, qi, ki: (b, qi, 0)),
                  pl.BlockSpec((1, N_HEADS // 2, tk, 2 * LANES), lambda b, qi, ki: (b, 0, kblk(qi, ki), 0)),
                  pl.BlockSpec((1, tk, BRANCH_W), lambda b, qi, ki: (b, kblk(qi, ki), 0))],
        out_specs=pl.BlockSpec((1, tq, BRANCH_W), lambda b, qi, ki: (b, qi, 0)),
        out_shape=jax.ShapeDtypeStruct((B, T, BRANCH_W), BF16),
        scratch_shapes=[pltpu.VMEM((N_HEADS, tq, 2 * LANES), BF16)] + _softmax_scratch(tq),
        compiler_params=_cparams("parallel", "parallel", "arbitrary"),
    )(qn, qr, kpp, v)


def _fox_attn_kernel(q_ref, k_ref, v_ref, cq_ref, ck_ref, o_ref, qs_ref, m_ref, l_ref, acc_ref,
                     *, tq, tk, past):
    qi = pl.program_id(1)
    ki = pl.program_id(2)
    q0 = past + qi * tq
    last_k = (q0 + tq - 1) // tk
    lane = lax.broadcasted_iota(I32, (tq, LANES), 1)

    @pl.when(ki == 0)
    def _init():
        _init_softmax(m_ref, l_ref, acc_ref)
        for h in range(N_HEADS):
            q = q_ref[0, :, (h // 2) * LANES:(h // 2 + 1) * LANES]
            qs_ref[h] = _keep_head(q, lane, h, HEAD_DIM)

    @pl.when(ki <= last_k)
    def _step():
        qpos = q0 + lax.broadcasted_iota(I32, (tq, tk), 0)
        kpos = ki * tk + lax.broadcasted_iota(I32, (tq, tk), 1)
        ok = kpos <= qpos
        for h in range(N_HEADS):
            p = h // 2
            s = _dot_nt(qs_ref[h], k_ref[0, :, p * LANES:(p + 1) * LANES])
            s = s + cq_ref[0, :, h:h + 1] - ck_ref[0, h:h + 1, :]
            s = jnp.where(ok, s, NEG)
            _softmax_step(h, s, v_ref[0, :, p * LANES:(p + 1) * LANES], m_ref, l_ref, acc_ref)

    @pl.when(ki == pl.num_programs(2) - 1)
    def _fin():
        _write_heads(o_ref, l_ref, acc_ref, lane)


def _fox_attention(q, k, v, cq, ck, past):
    B, T, _ = q.shape
    Tkp = k.shape[1]
    tq = _tile(T, 512)
    tk = _tile(Tkp, 512)

    def kblk(qi, ki):
        return jnp.minimum(ki, (past + qi * tq + tq - 1) // tk)

    kern = functools.partial(_fox_attn_kernel, tq=tq, tk=tk, past=past)
    return pl.pallas_call(
        kern, grid=(B, T // tq, Tkp // tk),
        in_specs=[pl.BlockSpec((1, tq, BRANCH_W), lambda b, qi, ki: (b, qi, 0)),
                  pl.BlockSpec((1, tk, BRANCH_W), lambda b, qi, ki: (b, kblk(qi, ki), 0)),
                  pl.BlockSpec((1, tk, BRANCH_W), lambda b, qi, ki: (b, kblk(qi, ki), 0)),
                  pl.BlockSpec((1, tq, N_HEADS), lambda b, qi, ki: (b, qi, 0)),
                  pl.BlockSpec((1, N_HEADS, tk), lambda b, qi, ki: (b, 0, kblk(qi, ki)))],
        out_specs=pl.BlockSpec((1, tq, BRANCH_W), lambda b, qi, ki: (b, qi, 0)),
        out_shape=jax.ShapeDtypeStruct((B, T, BRANCH_W), BF16),
        scratch_shapes=[pltpu.VMEM((N_HEADS, tq, LANES), BF16)] + _softmax_scratch(tq),
        compiler_params=_cparams("parallel", "parallel", "arbitrary"),
    )(q, k, v, cq, ck)


def _dsa_attn_kernel(q_ref, k_ref, v_ref, iq_ref, ikt_ref, iw_ref, o_ref,
                     key_ref, qs_ref, iqs_ref, m_ref, l_ref, acc_ref, cidx_ref,
                     *, tq, tk, past, n_keys, padded, n_keep, idx_bits):
    qi = pl.program_id(1)
    q0 = past + qi * tq
    adm_end = jnp.minimum(n_keys, ((q0 + tq - 1) // CHUNK + 1) * CHUNK)
    nkb = (adm_end + tk - 1) // tk
    lane = lax.broadcasted_iota(I32, (tq, LANES), 1)
    qpos = q0 + lax.broadcasted_iota(I32, (tq, tk), 0)
    col = lax.broadcasted_iota(I32, (tq, tk), 1)

    _init_softmax(m_ref, l_ref, acc_ref)
    for h in range(N_HEADS):
        q = q_ref[0, :, (h // 2) * LANES:(h // 2 + 1) * LANES]
        qs_ref[h] = _keep_head(q, lane, h, HEAD_DIM)
        iq = iq_ref[0, :, (h // 4) * LANES:(h // 4 + 1) * LANES]
        iqs_ref[h] = _keep_head(iq, lane, h, IDX_DIM)

    def score_body(kb, carry):
        ikt = ikt_ref[0, kb]
        sc = jnp.zeros((tq, tk), F32)
        for h in range(IDX_H):
            sc = sc + iw_ref[0, :, h:h + 1] * jnp.maximum(_dot(iqs_ref[h], ikt), 0.0)
        kpos = kb * tk + col
        bits = pltpu.bitcast(sc + 0.0, I32)
        okey = jnp.where(bits < 0, bits ^ 0x7FFFFFFF, bits)
        kchunk = jnp.right_shift(kpos, CHUNK_SHIFT)
        if padded:
            kchunk = jnp.where(kpos < n_keys, kchunk, BIG_IDX)
        adm = kchunk <= jnp.right_shift(qpos, CHUNK_SHIFT)
        key_ref[kb] = jnp.where(adm, okey, KEY_NEG_INF)
        return carry

    lax.fori_loop(0, nkb, score_body, 0)

    def count(pred):
        def body(kb, c):
            return c + jnp.sum(pred(key_ref[kb], kb), axis=1, keepdims=True)
        return lax.fori_loop(0, nkb, body, jnp.zeros((tq, 1), F32))

    def ones_where(cond):
        return jnp.where(cond, 1.0, 0.0)

    keep = float(n_keep)
    c_nonneg = count(lambda kt, kb: ones_where(kt >= 0))
    base = jnp.where(c_nonneg >= keep, jnp.zeros((tq, 1), I32), jnp.full((tq, 1), INT_MIN, I32))

    def bit_body(j, base):
        cand = base | jnp.left_shift(jnp.int32(1), 30 - j)
        c = count(lambda kt, kb: ones_where(kt >= cand))
        return jnp.where(c >= keep, cand, base)

    thr = lax.fori_loop(0, 31, bit_body, base)

    c_gt = count(lambda kt, kb: ones_where(kt > thr))
    c_ge = count(lambda kt, kb: ones_where(kt >= thr))
    need = keep - c_gt
    few = thr == KEY_NEG_INF
    tied = jnp.where(few, 0.0, ones_where(c_ge > keep))
    cidx_ref[...] = jnp.where(few, -1, BIG_IDX)

    @pl.when(jnp.max(tied) > 0.5)
    def _ties():
        def idx_body(j, b):
            cand = b | jnp.left_shift(jnp.int32(1), idx_bits - 1 - j)
            f = count(lambda kt, kb: jnp.where(kt == thr, ones_where(kb * tk + col < cand), 0.0))
            return jnp.where(f < need, cand, b)
        b = lax.fori_loop(0, idx_bits, idx_body, jnp.zeros((tq, 1), I32))
        cidx_ref[...] = jnp.where(few, -1, jnp.where(tied > 0.5, b, BIG_IDX))

    cidx = cidx_ref[...]

    def attn_body(kb, carry):
        k0 = pl.multiple_of(kb * tk, tk)
        kt = key_ref[kb]
        kpos = kb * tk + col
        bias = jnp.where(kt > thr, 0.0, jnp.where(kt == thr, jnp.where(kpos <= cidx, 0.0, NEG), NEG))
        for h in range(N_HEADS):
            p = h // 2
            s = _dot_nt(qs_ref[h], k_ref[0, pl.ds(k0, tk), p * LANES:(p + 1) * LANES]) + bias
            _softmax_step(h, s, v_ref[0, pl.ds(k0, tk), p * LANES:(p + 1) * LANES], m_ref, l_ref, acc_ref)
        return carry

    lax.fori_loop(0, nkb, attn_body, 0)
    _write_heads(o_ref, l_ref, acc_ref, lane)


def _dsa_attention(q, k, v, iq, ikt, iw, past, n_keys):
    B, T, _ = q.shape
    Tkp = k.shape[1]
    tq = _tile(T, 512)
    tk = _tile(Tkp, 512)
    nkb = Tkp // tk
    n_keep = min(TOPK_MAX, n_keys // 4)
    idx_bits = max(1, (Tkp - 1).bit_length())
    kern = functools.partial(_dsa_attn_kernel, tq=tq, tk=tk, past=past, n_keys=n_keys, padded=Tkp != n_keys,
                             n_keep=n_keep, idx_bits=idx_bits)
    return pl.pallas_call(
        kern, grid=(B, T // tq),
        in_specs=[pl.BlockSpec((1, tq, BRANCH_W), lambda b, qi: (b, qi, 0)),
                  pl.BlockSpec((1, Tkp, BRANCH_W), lambda b, qi: (b, 0, 0)),
                  pl.BlockSpec((1, Tkp, BRANCH_W), lambda b, qi: (b, 0, 0)),
                  pl.BlockSpec((1, tq, IDX_H * IDX_DIM), lambda b, qi: (b, qi, 0)),
                  pl.BlockSpec((1, nkb, LANES, tk), lambda b, qi: (b, 0, 0, 0)),
                  pl.BlockSpec((1, tq, IDX_H), lambda b, qi: (b, qi, 0))],
        out_specs=pl.BlockSpec((1, tq, BRANCH_W), lambda b, qi: (b, qi, 0)),
        out_shape=jax.ShapeDtypeStruct((B, T, BRANCH_W), BF16),
        scratch_shapes=[pltpu.VMEM((nkb, tq, tk), I32),
                        pltpu.VMEM((N_HEADS, tq, LANES), BF16),
                        pltpu.VMEM((IDX_H, tq, LANES), BF16)] + _softmax_scratch(tq)
                       + [pltpu.VMEM((tq, 1), I32)],
        compiler_params=_cparams("parallel", "arbitrary"),
    )(q, k, v, iq, ikt, iw)


def _merge_kernel(h_ref, hb_ref, oa_ref, ob_ref, oc_ref, wg_ref, bg_ref, wbr_ref, wout_ref, g_ref, b_ref,
                  wr_ref, br_ref, x_ref, xb_ref, idx_ref, gate_ref):
    hb = hb_ref[...]
    d = D_MODEL
    merged = None
    for n, o_ref in enumerate((oa_ref, ob_ref, oc_ref)):
        gt = _dot(hb, wg_ref[:, n * d:(n + 1) * d]) + bg_ref[:, n * d:(n + 1) * d]
        term = (1.0 / (1.0 + jnp.exp(-gt))) * _dot(o_ref[...], wbr_ref[n])
        merged = term if merged is None else merged + term
    out = _dot(merged.astype(BF16), wout_ref[...])
    x = _layer_norm(DN_ALPHA * h_ref[...] + out, g_ref[...], b_ref[...])
    xb = x.astype(BF16)
    x_ref[...] = x
    xb_ref[...] = xb

    logits = _dot(xb, wr_ref[...]) + br_ref[...]
    tm = logits.shape[0]
    lane = lax.broadcasted_iota(I32, (tm, N_EXPERTS), 1).astype(F32)
    k4 = lax.broadcasted_iota(I32, (tm, TOP_K), 1)
    vals = jnp.zeros((tm, TOP_K), F32)
    idxs = jnp.zeros((tm, TOP_K), F32)
    for k in range(TOP_K):
        mx = jnp.max(logits, axis=1, keepdims=True)
        ix = jnp.min(jnp.where(logits == mx, lane, float(N_EXPERTS)), axis=1, keepdims=True)
        vals = jnp.where(k4 == k, mx, vals)
        idxs = jnp.where(k4 == k, ix, idxs)
        logits = jnp.where(lane == ix, -jnp.inf, logits)
    e = jnp.exp(vals - jnp.max(vals, axis=1, keepdims=True))
    idx_ref[...] = idxs.astype(I32)
    gate_ref[...] = e * (1.0 / jnp.sum(e, axis=1, keepdims=True))


def _merge(h, hb, oa, ob, oc, wg, bg, wbr, wout, g, b, wr, br):
    n, d = h.shape
    tm = _tile(n, 256)
    row = lambda w: pl.BlockSpec((tm, w), lambda i: (i, 0))
    return pl.pallas_call(
        _merge_kernel, grid=(n // tm,),
        in_specs=[row(d), row(d), row(BRANCH_W), row(BRANCH_W), row(BRANCH_W),
                  _full(wg.shape), _full(bg.shape), _full(wbr.shape), _full(wout.shape),
                  _full(g.shape), _full(b.shape), _full(wr.shape), _full(br.shape)],
        out_specs=[row(d), row(d), row(TOP_K), row(TOP_K)],
        out_shape=[jax.ShapeDtypeStruct((n, d), F32), jax.ShapeDtypeStruct((n, d), BF16),
                   jax.ShapeDtypeStruct((n, TOP_K), I32), jax.ShapeDtypeStruct((n, TOP_K), F32)],
        compiler_params=_cparams("parallel"),
    )(h, hb, oa, ob, oc, wg, bg, wbr, wout, g, b, wr, br)


def _expert_kernel(blk_e_ref, idx_hbm, x_hbm, w1_ref, b1_ref, w2_ref, b2_ref, y_hbm,
                   idx_smem, xbuf, ybuf, isem, gsem, ssem, *, bs, n_blocks):
    del blk_e_ref
    i = pl.program_id(0)
    slot = i % 2
    nslot = 1 - slot

    def idx_copy(blk, s):
        return pltpu.make_async_copy(idx_hbm.at[blk], idx_smem.at[s], isem.at[s])

    def start_gather(s):
        def body(r, c):
            tok = idx_smem[s, r]
            pltpu.make_async_copy(x_hbm.at[pl.ds(tok, 1)], xbuf.at[s, pl.ds(r, 1)], gsem.at[s]).start()
            return c
        lax.fori_loop(0, bs, body, 0, unroll=8)

    def wait_gather(s):
        pltpu.make_async_copy(x_hbm.at[pl.ds(0, bs)], xbuf.at[s], gsem.at[s]).wait()

    def start_scatter(s):
        def body(r, c):
            dst = idx_smem[s, bs + r]
            pltpu.make_async_copy(ybuf.at[s, pl.ds(r, 1)], y_hbm.at[pl.ds(dst, 1)], ssem.at[s]).start()
            return c
        lax.fori_loop(0, bs, body, 0, unroll=8)

    def wait_scatter(s):
        pltpu.make_async_copy(ybuf.at[s], y_hbm.at[pl.ds(0, bs)], ssem.at[s]).wait()

    @pl.when(i == 0)
    def _prime():
        idx_copy(0, 0).start()
        idx_copy(0, 0).wait()
        start_gather(0)
        if n_blocks > 1:
            idx_copy(1, 1).start()

    @pl.when(i + 1 < n_blocks)
    def _prefetch():
        idx_copy(i + 1, nslot).wait()
        start_gather(nslot)

    wait_gather(slot)

    @pl.when(i >= 2)
    def _():
        wait_scatter(slot)

    x = xbuf[slot].astype(BF16)
    gu = _dot(x, w1_ref[0]) + b1_ref[0]
    g = jnp.minimum(gu[:, :D_FF], SWIGLU_LIMIT)
    u = jnp.clip(gu[:, D_FF:], -SWIGLU_LIMIT, SWIGLU_LIMIT)
    a = g * (1.0 / (1.0 + jnp.exp(-SWIGLU_ALPHA * g))) * (u + 1.0)
    ybuf[slot] = _dot(a.astype(BF16), w2_ref[0]) + b2_ref[0]
    start_scatter(slot)

    @pl.when(i + 2 < n_blocks)
    def _():
        idx_copy(i + 2, slot).start()

    @pl.when(i == n_blocks - 1)
    def _drain():
        if n_blocks > 1:
            wait_scatter(nslot)
        wait_scatter(slot)


def _experts(x, blk_e, idx, w1, b1, w2, b2, n_rows_out, bs):
    n, d = x.shape
    n_blocks = idx.shape[0]
    kern = functools.partial(_expert_kernel, bs=bs, n_blocks=n_blocks)
    grid_spec = pltpu.PrefetchScalarGridSpec(
        num_scalar_prefetch=1, grid=(n_blocks,),
        in_specs=[pl.BlockSpec(memory_space=pl.ANY), pl.BlockSpec(memory_space=pl.ANY),
                  pl.BlockSpec((1, d, 2 * D_FF), lambda i, be: (be[i], 0, 0)),
                  pl.BlockSpec((1, 1, 2 * D_FF), lambda i, be: (be[i], 0, 0)),
                  pl.BlockSpec((1, D_FF, d), lambda i, be: (be[i], 0, 0)),
                  pl.BlockSpec((1, 1, d), lambda i, be: (be[i], 0, 0))],
        out_specs=pl.BlockSpec(memory_space=pl.ANY),
        scratch_shapes=[pltpu.SMEM((2, 2 * bs), I32),
                        pltpu.VMEM((2, bs, d), F32), pltpu.VMEM((2, bs, d), F32),
                        pltpu.SemaphoreType.DMA((2,)), pltpu.SemaphoreType.DMA((2,)),
                        pltpu.SemaphoreType.DMA((2,))])
    return pl.pallas_call(
        kern, grid_spec=grid_spec,
        out_shape=jax.ShapeDtypeStruct((n_rows_out, d), F32),
        compiler_params=_cparams("arbitrary"),
    )(blk_e, idx, x, w1, b1, w2, b2)


def _dispatch_tables(top_idx, bs):
    n = top_idx.shape[0]
    m = n * TOP_K
    flat_e = top_idx.reshape(-1)
    order = jnp.argsort(flat_e).astype(I32)
    se = flat_e[order]
    counts = jnp.bincount(flat_e, length=N_EXPERTS).astype(I32)
    padded = (counts + bs - 1) // bs * bs
    start = jnp.cumsum(counts) - counts
    pend = jnp.cumsum(padded)
    pstart = pend - padded
    dest = pstart[se] + jnp.arange(m, dtype=I32) - start[se]
    n_blocks = -(-m // bs) + N_EXPERTS
    cap = n_blocks * bs
    slot_flat = jnp.full((cap,), -1, I32).at[dest].set(order)
    is_pad = slot_flat < 0
    src = jnp.where(is_pad, 0, slot_flat // TOP_K)
    dst = jnp.where(is_pad, m - 1 + jnp.cumsum(is_pad.astype(I32)), slot_flat)
    blk_e = jnp.minimum(jnp.searchsorted(pend, jnp.arange(n_blocks, dtype=I32) * bs, side='right'),
                        N_EXPERTS - 1).astype(I32)
    idx = jnp.concatenate([src.reshape(n_blocks, bs), dst.reshape(n_blocks, bs)], axis=1)
    return blk_e, idx, cap


def _combine_kernel(x_ref, y4_ref, gate_ref, g_ref, b_ref, o_ref, ob_ref):
    d = D_MODEL
    y = gate_ref[:, 0:1] * y4_ref[:, :d]
    for k in range(1, TOP_K):
        y = y + gate_ref[:, k:k + 1] * y4_ref[:, k * d:(k + 1) * d]
    out = _layer_norm(DN_ALPHA * x_ref[...] + y, g_ref[...], b_ref[...])
    o_ref[...] = out
    ob_ref[...] = out.astype(BF16)


def _combine(x, y4, gate, g, b):
    n, d = x.shape
    tm = _tile(n, 256)
    row = lambda w: pl.BlockSpec((tm, w), lambda i: (i, 0))
    return pl.pallas_call(
        _combine_kernel, grid=(n // tm,),
        in_specs=[row(d), row(TOP_K * d), row(TOP_K), _full((1, d)), _full((1, d))],
        out_specs=[row(d), row(d)],
        out_shape=[jax.ShapeDtypeStruct((n, d), F32), jax.ShapeDtypeStruct((n, d), BF16)],
        compiler_params=_cparams("parallel"),
    )(x, y4, gate, g, b)


def _prep_layer(l, w_in, b_f, b_gate, g_qa, g_kva, w_uq, w_ukv, w_br, w_out, ln1_g, ln1_b,
                w_router, b_router, w_up, b_up, w_down, b_down, ln2_g, ln2_b):
    w = w_in[l]
    o = [0]

    def take(width):
        s = w[:, o[0]:o[0] + width]
        o[0] += width
        return s

    def pad_to(a, width):
        return jnp.pad(a, ((0, 0), (0, width - a.shape[1])))

    qa, kva, kra = take(Q_LORA), take(KV_LORA), take(ROPE_A)
    qb, kb, vb, fb = take(BRANCH_W), take(BRANCH_W), take(BRANCH_W), take(N_HEADS)
    qc, kc, vc = take(BRANCH_W), take(BRANCH_W), take(BRANCH_W)
    iqc, ikc, iwc = take(IDX_H * IDX_DIM), take(IDX_DIM), take(IDX_H)
    gt = take(N_BRANCH * D_MODEL)
    p = {}
    p['w_mla'] = jnp.concatenate([qa, kva, jnp.tile(kra, (1, 4))], axis=1).astype(BF16)
    p['w_fox'] = jnp.concatenate([qb, kb, vb, pad_to(fb, LANES)], axis=1).astype(BF16)
    p['w_dsa'] = jnp.concatenate([qc, kc, vc, iqc, jnp.tile(ikc, (1, 4)), pad_to(iwc, LANES)], axis=1).astype(BF16)
    p['w_gate'] = gt.astype(BF16)
    uq = w_uq[l].reshape(Q_LORA, N_HEADS, HEAD_DIM + ROPE_A)
    p['w_uq'] = jnp.concatenate([uq[:, :, :HEAD_DIM].reshape(Q_LORA, -1),
                                 uq[:, :, HEAD_DIM:].reshape(Q_LORA, -1)], axis=1).astype(BF16)
    ukv = w_ukv[l].reshape(KV_LORA, N_HEADS, 2 * HEAD_DIM)
    p['w_ukv'] = jnp.concatenate([ukv[:, :, :HEAD_DIM].reshape(KV_LORA, -1),
                                  ukv[:, :, HEAD_DIM:].reshape(KV_LORA, -1)], axis=1).astype(BF16)
    p['b_f'] = b_f[l].reshape(1, N_HEADS)
    p['b_gate'] = b_gate[l].reshape(1, N_BRANCH * D_MODEL)
    p['g_qa'] = g_qa[l].reshape(1, Q_LORA)
    p['g_kva'] = g_kva[l].reshape(1, KV_LORA)
    p['w_br'] = w_br[l].astype(BF16)
    p['w_out'] = w_out[l].astype(BF16)
    p['ln1'] = (ln1_g[l].reshape(1, -1), ln1_b[l].reshape(1, -1))
    p['ln2'] = (ln2_g[l].reshape(1, -1), ln2_b[l].reshape(1, -1))
    p['w_router'] = w_router[l].astype(BF16)
    p['b_router'] = b_router[l].reshape(1, N_EXPERTS)
    p['w_up'] = jnp.concatenate([w_up[l][:, :, 0::2], w_up[l][:, :, 1::2]], axis=2).astype(BF16)
    p['b_up'] = jnp.concatenate([b_up[l][:, 0::2], b_up[l][:, 1::2]], axis=1)[:, None, :]
    p['w_down'] = w_down[l].astype(BF16)
    p['b_down'] = b_down[l][:, None, :]
    return p


def _pad_time(a, t_pad):
    return a if a.shape[1] == t_pad else jnp.pad(a, ((0, 0), (0, t_pad - a.shape[1])) + ((0, 0),) * (a.ndim - 2))


def _trunk(x, cache, params, ln_in, moe_block):
    B, T, D = x.shape
    n = B * T
    past = 0 if cache is None else cache[0].shape[2]
    n_keys = past + T
    tkp = -(-n_keys // LANES) * LANES
    pos = jnp.arange(past, past + T, dtype=I32)
    tab_a = _rope_tables(pos, ROPE_A, ROPE_A)
    tab_c = _rope_tables(pos, ROT_C, HEAD_DIM)
    tab_i = _rope_tables(pos, ROT_IDX, IDX_DIM)
    tile_mat = (jnp.arange(ROPE_A)[:, None] == (jnp.arange(LANES)[None, :] % ROPE_A)).astype(BF16)
    tm = _tile(T, 512)

    h, hb = _ln_in(x.reshape(n, D), *ln_in)
    rows = [[] for _ in range(8)]
    for l, p in enumerate(params):
        hb3 = hb.reshape(B, T, D)
        qn, qr, ckv_new, kr_new = _proj_call(
            _mla_proj_kernel, hb3, [p['w_mla'], p['w_uq'], p['g_qa'], p['g_kva']], tab_a,
            [(BRANCH_W, BF16), (2 * LANES, BF16), (KV_LORA, F32), (ROPE_A, F32)], tm)
        qb, kb_new, vb_new, kb16, vb16, lf_new = _proj_call(
            _fox_proj_kernel, hb3, [p['w_fox'], p['b_f']], (),
            [(BRANCH_W, BF16), (BRANCH_W, F32), (BRANCH_W, F32), (BRANCH_W, BF16), (BRANCH_W, BF16),
             (N_HEADS, F32)], tm)
        qc, kc_new, vc_new, kc16, vc16, iq, ik_new, iw = _proj_call(
            _dsa_proj_kernel, hb3, [p['w_dsa']], tab_c + tab_i,
            [(BRANCH_W, BF16), (BRANCH_W, F32), (BRANCH_W, F32), (BRANCH_W, BF16), (BRANCH_W, BF16),
             (IDX_H * IDX_DIM, BF16), (IDX_DIM, F32), (IDX_H, F32)], tm)

        if cache is None:
            ckv_all, kr_all, lf_all, ik_all = ckv_new, kr_new, lf_new, ik_new
            kb_all, vb_all, kc_all, vc_all = kb16, vb16, kc16, vc16
        else:
            c_ckv, c_kr, c_kb, c_vb, c_lf, c_kc, c_vc, c_ik = (c[l] for c in cache)
            flat = lambda c: c.reshape(B, past, BRANCH_W).astype(BF16)
            ckv_all = jnp.concatenate([c_ckv, ckv_new], axis=1)
            kr_all = jnp.concatenate([c_kr, kr_new], axis=1)
            lf_all = jnp.concatenate([c_lf, lf_new], axis=1)
            ik_all = jnp.concatenate([c_ik, ik_new], axis=1)
            kb_all = jnp.concatenate([flat(c_kb), kb16], axis=1)
            vb_all = jnp.concatenate([flat(c_vb), vb16], axis=1)
            kc_all = jnp.concatenate([flat(c_kc), kc16], axis=1)
            vc_all = jnp.concatenate([flat(c_vc), vc16], axis=1)
        ckv_all, kr_all, lf_all, ik_all, kb_all, vb_all, kc_all, vc_all = (
            _pad_time(a, tkp) for a in (ckv_all, kr_all, lf_all, ik_all, kb_all, vb_all, kc_all, vc_all))

        kpp, v_a = _mla_kv(ckv_all, kr_all, p['w_ukv'], tile_mat)
        o_a = _mla_attention(qn, qr, kpp, v_a, past, n_keys)
        cum = _cumsum_time(jnp.swapaxes(lf_all, 1, 2))
        cq = jnp.swapaxes(cum[:, :, past:past + T], 1, 2)
        o_b = _fox_attention(qb, kb_all, vb_all, cq, cum, past)
        tk = _tile(tkp, 512)
        ikt = jnp.swapaxes(ik_all.astype(BF16), 1, 2)
        ikt = jnp.tile(ikt, (1, LANES // IDX_DIM, 1))
        ikt = jnp.swapaxes(ikt.reshape(B, LANES, tkp // tk, tk), 1, 2)
        o_c = _dsa_attention(qc, kc_all, vc_all, iq, ikt, iw, past, n_keys)

        x1, x1b, top_idx, gate = _merge(
            h, hb, o_a.reshape(n, -1), o_b.reshape(n, -1), o_c.reshape(n, -1),
            p['w_gate'], p['b_gate'], p['w_br'], p['w_out'], *p['ln1'], p['w_router'], p['b_router'])

        blk_e, idx, n_rows = _dispatch_tables(top_idx, moe_block)
        y4 = _experts(x1, blk_e, idx, p['w_up'], p['b_up'], p['w_down'], p['b_down'], n_rows, moe_block)
        h, hb = _combine(x1, y4.reshape(n_rows // TOP_K, TOP_K * D), gate, *p['ln2'])

        new = (ckv_new, kr_new, kb_new.reshape(B, T, N_HEADS, HEAD_DIM), vb_new.reshape(B, T, N_HEADS, HEAD_DIM),
               lf_new, kc_new.reshape(B, T, N_HEADS, HEAD_DIM), vc_new.reshape(B, T, N_HEADS, HEAD_DIM), ik_new)
        for acc, r in zip(rows, new):
            acc.append(r)
    return h.reshape(B, T, D), [jnp.stack(a) for a in rows]


def kernel(x_prompt, x_sample, cache_mla_ckv, cache_mla_krope, cache_fox_k, cache_fox_v, cache_fox_logf,
           cache_dsa_k, cache_dsa_v, cache_dsa_idxk, ln_in_g, ln_in_b, w_in, b_f, b_gate, g_qa, g_kva,
           w_uq, w_ukv, w_br, w_out, ln1_g, ln1_b, w_router, b_router, w_up, b_up, w_down, b_down,
           ln2_g, ln2_b):
    depth = w_in.shape[0]
    params = [_prep_layer(l, w_in, b_f, b_gate, g_qa, g_kva, w_uq, w_ukv, w_br, w_out, ln1_g, ln1_b,
                          w_router, b_router, w_up, b_up, w_down, b_down, ln2_g, ln2_b) for l in range(depth)]
    ln_in = (ln_in_g, ln_in_b)
    y_p, st_p = _trunk(x_prompt, None, params, ln_in, moe_block=512)
    caches = (cache_mla_ckv, cache_mla_krope, cache_fox_k, cache_fox_v, cache_fox_logf,
              cache_dsa_k, cache_dsa_v, cache_dsa_idxk)
    y_s, st_s = _trunk(x_sample, caches, params, ln_in, moe_block=128)
    out = [y_p, y_s]
    for a, b in zip(st_p, st_s):
        out += [a, b]
    return tuple(out)
```

```python
import functools

import jax
import jax.numpy as jnp
from jax import lax
from jax.experimental import pallas as pl
from jax.experimental.pallas import tpu as pltpu

F32 = jnp.float32
BF16 = jnp.bfloat16
I32 = jnp.int32

D_MODEL = 1024
CHUNK = 64
CHUNK_SHIFT = 6
ROPE_THETA = 500000.0
LN_EPS = 1e-5
RMS_EPS = 1e-6
DEPTH_NOMINAL = 4
DN_ALPHA = (2 * DEPTH_NOMINAL) ** 0.25
N_HEADS = 8
HEAD_DIM = 64
Q_LORA = 384
KV_LORA = 256
ROPE_A = 32
ROT_C = 16
IDX_H = 8
IDX_DIM = 32
ROT_IDX = 8
TOPK_MAX = 256
N_BRANCH = 3
BRANCH_W = N_HEADS * HEAD_DIM
N_EXPERTS = 32
TOP_K = 4
D_FF = 1024
SWIGLU_LIMIT = 7.0
SWIGLU_ALPHA = 1.702

LANES = 128
VMEM_LIMIT_BYTES = 56 * 2**20

NEG = -1e30
INT_MIN = -2**31
KEY_NEG_INF = -2139095041
BIG_IDX = 2**30


def _cparams(*sem):
    return pltpu.CompilerParams(dimension_semantics=sem, vmem_limit_bytes=VMEM_LIMIT_BYTES)


def _tile(n, pref):
    return pref if n % pref == 0 else n


def _full(shape):
    zeros = (0,) * len(shape)
    return pl.BlockSpec(shape, lambda *_: zeros)


def _dot(a, b):
    return jnp.dot(a, b, preferred_element_type=F32)


def _dot_nt(a, b):
    return lax.dot_general(a, b, (((1,), (1,)), ((), ())), preferred_element_type=F32)


def _layer_norm(x, g, b):
    xc = x - jnp.mean(x, axis=-1, keepdims=True)
    var = jnp.mean(xc * xc, axis=-1, keepdims=True)
    return xc * lax.rsqrt(var + LN_EPS) * g + b


def _rms_norm(x, g):
    return x * lax.rsqrt(jnp.mean(x * x, axis=-1, keepdims=True) + RMS_EPS) * g


def _rope_slab(x, c, sa, sb, half):
    return x * c + pltpu.roll(x, half, 1) * sa + pltpu.roll(x, LANES - half, 1) * sb


def _rope_tables(pos, rot, period):
    half = rot // 2
    inv_freq = ROPE_THETA ** (-jnp.arange(half, dtype=F32) / half)
    ang = pos.astype(F32)[:, None] * inv_freq
    j = jnp.arange(LANES) % period
    cos = jnp.cos(ang)[:, j % half]
    sin = jnp.sin(ang)[:, j % half]
    c = jnp.where(j < rot, cos, 1.0)
    sa = jnp.where((j >= half) & (j < rot), sin, 0.0)
    sb = jnp.where(j < half, -sin, 0.0)
    return c, sa, sb


def _ln_kernel(x_ref, g_ref, b_ref, o_ref, ob_ref):
    y = _layer_norm(x_ref[...], g_ref[...], b_ref[...])
    o_ref[...] = y
    ob_ref[...] = y.astype(BF16)


def _ln_in(x, g, b):
    n, d = x.shape
    tm = _tile(n, 512)
    row = pl.BlockSpec((tm, d), lambda i: (i, 0))
    return pl.pallas_call(
        _ln_kernel, grid=(n // tm,),
        in_specs=[row, _full((1, d)), _full((1, d))],
        out_specs=[row, row],
        out_shape=[jax.ShapeDtypeStruct((n, d), F32), jax.ShapeDtypeStruct((n, d), BF16)],
        compiler_params=_cparams("parallel"),
    )(x, g.reshape(1, d), b.reshape(1, d))


def _mla_proj_kernel(hb_ref, w_ref, wuq_ref, gqa_ref, gkva_ref, c_ref, sa_ref, sb_ref,
                     qn_ref, qr_ref, ckv_ref, kr_ref):
    z = _dot(hb_ref[0], w_ref[...])
    qa = _rms_norm(z[:, :Q_LORA], gqa_ref[...])
    q = _dot(qa.astype(BF16), wuq_ref[...])
    qn_ref[0] = q[:, :BRANCH_W].astype(BF16)
    c, sa, sb = c_ref[...], sa_ref[...], sb_ref[...]
    for s in range(2):
        lo = BRANCH_W + s * LANES
        qr_ref[0, :, s * LANES:(s + 1) * LANES] = _rope_slab(q[:, lo:lo + LANES], c, sa, sb, ROPE_A // 2).astype(BF16)
    ckv_ref[0] = _rms_norm(z[:, Q_LORA:Q_LORA + KV_LORA], gkva_ref[...])
    kr = _rope_slab(z[:, Q_LORA + KV_LORA:], c, sa, sb, ROPE_A // 2)
    kr_ref[0] = kr[:, :ROPE_A]


def _fox_proj_kernel(hb_ref, w_ref, bf_ref, q_ref, k_ref, v_ref, k16_ref, v16_ref, lf_ref):
    z = _dot(hb_ref[0], w_ref[...])
    w = BRANCH_W
    q_ref[0] = (z[:, :w] * HEAD_DIM ** -0.5).astype(BF16)
    k = z[:, w:2 * w]
    v = z[:, 2 * w:3 * w]
    k_ref[0] = k
    v_ref[0] = v
    k16_ref[0] = k.astype(BF16)
    v16_ref[0] = v.astype(BF16)
    x = z[:, 3 * w:3 * w + N_HEADS] + bf_ref[...]
    lf_ref[0] = jnp.minimum(x, 0.0) - jnp.log1p(jnp.exp(-jnp.abs(x)))


def _dsa_proj_kernel(hb_ref, w_ref, c64_ref, sa64_ref, sb64_ref, c32_ref, sa32_ref, sb32_ref,
                     q_ref, k_ref, v_ref, k16_ref, v16_ref, iq_ref, ik_ref, iw_ref):
    z = _dot(hb_ref[0], w_ref[...])
    w = BRANCH_W
    c64, sa64, sb64 = c64_ref[...], sa64_ref[...], sb64_ref[...]
    c32, sa32, sb32 = c32_ref[...], sa32_ref[...], sb32_ref[...]
    for s in range(w // LANES):
        sl = slice(s * LANES, (s + 1) * LANES)
        qs = _rope_slab(z[:, s * LANES:(s + 1) * LANES], c64, sa64, sb64, ROT_C // 2)
        q_ref[0, :, sl] = (qs * HEAD_DIM ** -0.5).astype(BF16)
        ks = _rope_slab(z[:, w + s * LANES:w + (s + 1) * LANES], c64, sa64, sb64, ROT_C // 2)
        k_ref[0, :, sl] = ks
        k16_ref[0, :, sl] = ks.astype(BF16)
    v = z[:, 2 * w:3 * w]
    v_ref[0] = v
    v16_ref[0] = v.astype(BF16)
    o = 3 * w
    for s in range(2):
        iq_ref[0, :, s * LANES:(s + 1) * LANES] = _rope_slab(
            z[:, o + s * LANES:o + (s + 1) * LANES], c32, sa32, sb32, ROT_IDX // 2).astype(BF16)
    o += IDX_H * IDX_DIM
    ik = _rope_slab(z[:, o:o + LANES], c32, sa32, sb32, ROT_IDX // 2)
    ik_ref[0] = ik[:, :IDX_DIM]
    o += LANES
    iw_ref[0] = z[:, o:o + IDX_H] * (IDX_DIM ** -0.5 * IDX_H ** -0.5)


def _proj_call(kernel, hb, weights, tables, out_widths_dtypes, tm):
    B, T, D = hb.shape
    grid = (B, T // tm)
    in_specs = [pl.BlockSpec((1, tm, D), lambda b, t: (b, t, 0))]
    in_specs += [_full(w.shape) for w in weights]
    in_specs += [pl.BlockSpec((tm, LANES), lambda b, t: (t, 0)) for _ in tables]
    out_specs = [pl.BlockSpec((1, tm, w), lambda b, t: (b, t, 0)) for w, _ in out_widths_dtypes]
    out_shape = [jax.ShapeDtypeStruct((B, T, w), dt) for w, dt in out_widths_dtypes]
    return pl.pallas_call(kernel, grid=grid, in_specs=in_specs, out_specs=out_specs, out_shape=out_shape,
                          compiler_params=_cparams("parallel", "parallel"))(hb, *weights, *tables)


def _mla_kv_kernel(ckv_ref, kr_ref, w_ref, tile_ref, kpp_ref, v_ref):
    kv = _dot(ckv_ref[0].astype(BF16), w_ref[...])
    kr4 = _dot(kr_ref[0].astype(BF16), tile_ref[...]).astype(BF16)
    for p in range(N_HEADS // 2):
        kpp_ref[0, p, :, :LANES] = kv[:, p * LANES:(p + 1) * LANES].astype(BF16)
        kpp_ref[0, p, :, LANES:] = kr4
    v_ref[0] = kv[:, BRANCH_W:].astype(BF16)


def _mla_kv(ckv, kr, w_ukv_p, tile_mat):
    B, Tk, _ = ckv.shape
    tm = _tile(Tk, 512)
    return pl.pallas_call(
        _mla_kv_kernel, grid=(B, Tk // tm),
        in_specs=[pl.BlockSpec((1, tm, KV_LORA), lambda b, t: (b, t, 0)),
                  pl.BlockSpec((1, tm, ROPE_A), lambda b, t: (b, t, 0)),
                  _full(w_ukv_p.shape), _full(tile_mat.shape)],
        out_specs=[pl.BlockSpec((1, N_HEADS // 2, tm, 2 * LANES), lambda b, t: (b, 0, t, 0)),
                   pl.BlockSpec((1, tm, BRANCH_W), lambda b, t: (b, t, 0))],
        out_shape=[jax.ShapeDtypeStruct((B, N_HEADS // 2, Tk, 2 * LANES), BF16),
                   jax.ShapeDtypeStruct((B, Tk, BRANCH_W), BF16)],
        compiler_params=_cparams("parallel", "parallel"),
    )(ckv, kr, w_ukv_p, tile_mat)


def _cumsum_kernel(x_ref, u_ref, o_ref, carry_ref):
    @pl.when(pl.program_id(1) == 0)
    def _():
        carry_ref[...] = jnp.zeros_like(carry_ref)

    x = x_ref[0]
    u = u_ref[...]
    hi = x.astype(BF16)
    r1 = x - hi.astype(F32)
    mid = r1.astype(BF16)
    lo = (r1 - mid.astype(F32)).astype(BF16)
    out = _dot(hi, u) + _dot(mid, u) + _dot(lo, u) + carry_ref[...]
    o_ref[0] = out
    tc = x.shape[1]
    carry_ref[...] = out[:, tc - 1:tc]


def _cumsum_time(x):
    B, H, Tk = x.shape
    tc = _tile(Tk, 512)
    u = (jnp.arange(tc)[:, None] <= jnp.arange(tc)[None, :]).astype(BF16)
    return pl.pallas_call(
        _cumsum_kernel, grid=(B, Tk // tc),
        in_specs=[pl.BlockSpec((1, H, tc), lambda b, t: (b, 0, t)), _full((tc, tc))],
        out_specs=pl.BlockSpec((1, H, tc), lambda b, t: (b, 0, t)),
        out_shape=jax.ShapeDtypeStruct((B, H, Tk), F32),
        scratch_shapes=[pltpu.VMEM((H, 1), F32)],
        compiler_params=_cparams("parallel", "arbitrary"),
    )(x, u)


def _own_lanes(lane, h, width):
    per = LANES // width
    return jnp.right_shift(lane, width.bit_length() - 1) == (h % per)


def _keep_head(x, lane, h, width):
    return jnp.where(_own_lanes(lane, h, width), x.astype(F32), 0.0).astype(BF16)


def _softmax_step(h, s, v_slab, m_ref, l_ref, acc_ref):
    m_prev = m_ref[h]
    m_new = jnp.maximum(m_prev, jnp.max(s, axis=1, keepdims=True))
    alpha = jnp.exp(m_prev - m_new)
    p = jnp.exp(s - m_new)
    l_ref[h] = alpha * l_ref[h] + jnp.sum(p, axis=1, keepdims=True)
    acc_ref[h] = alpha * acc_ref[h] + _dot(p.astype(BF16), v_slab)
    m_ref[h] = m_new


def _init_softmax(m_ref, l_ref, acc_ref):
    m_ref[...] = jnp.full(m_ref.shape, NEG, F32)
    l_ref[...] = jnp.zeros(l_ref.shape, F32)
    acc_ref[...] = jnp.zeros(acc_ref.shape, F32)


def _write_heads(o_ref, l_ref, acc_ref, lane):
    for p in range(N_HEADS // 2):
        a = acc_ref[2 * p] * (1.0 / l_ref[2 * p])
        b = acc_ref[2 * p + 1] * (1.0 / l_ref[2 * p + 1])
        o_ref[0, :, p * LANES:(p + 1) * LANES] = jnp.where(lane < HEAD_DIM, a, b).astype(BF16)


def _softmax_scratch(tq):
    return [pltpu.VMEM((N_HEADS, tq, 1), F32), pltpu.VMEM((N_HEADS, tq, 1), F32),
            pltpu.VMEM((N_HEADS, tq, LANES), F32)]


def _mla_attn_kernel(qn_ref, qr_ref, kpp_ref, v_ref, o_ref, qs_ref, m_ref, l_ref, acc_ref,
                     *, tq, tk, past, n_keys, padded):
    qi = pl.program_id(1)
    ki = pl.program_id(2)
    q0 = past + qi * tq
    last_k = (jnp.minimum(n_keys, ((q0 + tq - 1) // CHUNK + 1) * CHUNK) - 1) // tk
    lane = lax.broadcasted_iota(I32, (tq, LANES), 1)

    @pl.when(ki == 0)
    def _init():
        _init_softmax(m_ref, l_ref, acc_ref)
        for h in range(N_HEADS):
            qn = qn_ref[0, :, (h // 2) * LANES:(h // 2 + 1) * LANES]
            qr = qr_ref[0, :, (h // 4) * LANES:(h // 4 + 1) * LANES]
            qs_ref[h, :, :LANES] = _keep_head(qn, lane, h, HEAD_DIM)
            qs_ref[h, :, LANES:] = _keep_head(qr, lane, h, ROPE_A)

    @pl.when(ki <= last_k)
    def _step():
        qpos = q0 + lax.broadcasted_iota(I32, (tq, tk), 0)
        kpos = ki * tk + lax.broadcasted_iota(I32, (tq, tk), 1)
        kchunk = jnp.right_shift(kpos, CHUNK_SHIFT)
        if padded:
            kchunk = jnp.where(kpos < n_keys, kchunk, BIG_IDX)
        ok = kchunk <= jnp.right_shift(qpos, CHUNK_SHIFT)
        scale = (HEAD_DIM + ROPE_A) ** -0.5
        for h in range(N_HEADS):
            p = h // 2
            s = _dot_nt(qs_ref[h], kpp_ref[0, p]) * scale
            s = jnp.where(ok, s, NEG)
            _softmax_step(h, s, v_ref[0, :, p * LANES:(p + 1) * LANES], m_ref, l_ref, acc_ref)

    @pl.when(ki == pl.num_programs(2) - 1)
    def _fin():
        _write_heads(o_ref, l_ref, acc_ref, lane)


def _mla_attention(qn, qr, kpp, v, past, n_keys):
    B, T, _ = qn.shape
    Tkp = v.shape[1]
    tq = _tile(T, 512)
    tk = _tile(Tkp, 512)

    def kblk(qi, ki):
        last = (jnp.minimum(n_keys, ((past + qi * tq + tq - 1) // CHUNK + 1) * CHUNK) - 1) // tk
        return jnp.minimum(ki, last)

    kern = functools.partial(_mla_attn_kernel, tq=tq, tk=tk, past=past, n_keys=n_keys, padded=Tkp != n_keys)
    return pl.pallas_call(
        kern, grid=(B, T // tq, Tkp // tk),
        in_specs=[pl.BlockSpec((1, tq, BRANCH_W), lambda b, qi, ki: (b, qi, 0)),
                  pl.BlockSpec((1, tq, 2 * LANES), lambda b, qi, ki: (b, qi, 0)),
                  pl.BlockSpec((1, N_HEADS // 2, tk, 2 * LANES), lambda b, qi, ki: (b, 0, kblk(qi, ki), 0)),
                  pl.BlockSpec((1, tk, BRANCH_W), lambda b, qi, ki: (b, kblk(qi, ki), 0))],
        out_specs=pl.BlockSpec((1, tq, BRANCH_W), lambda b, qi, ki: (b, qi, 0)),
        out_shape=jax.ShapeDtypeStruct((B, T, BRANCH_W), BF16),
        scratch_shapes=[pltpu.VMEM((N_HEADS, tq, 2 * LANES), BF16)] + _softmax_scratch(tq),
        compiler_params=_cparams("parallel", "parallel", "arbitrary"),
    )(qn, qr, kpp, v)


def _fox_attn_kernel(q_ref, k_ref, v_ref, cq_ref, ck_ref, o_ref, qs_ref, m_ref, l_ref, acc_ref,
                     *, tq, tk, past):
    qi = pl.program_id(1)
    ki = pl.program_id(2)
    q0 = past + qi * tq
    last_k = (q0 + tq - 1) // tk
    lane = lax.broadcasted_iota(I32, (tq, LANES), 1)

    @pl.when(ki == 0)
    def _init():
        _init_softmax(m_ref, l_ref, acc_ref)
        for h in range(N_HEADS):
            q = q_ref[0, :, (h // 2) * LANES:(h // 2 + 1) * LANES]
            qs_ref[h] = _keep_head(q, lane, h, HEAD_DIM)

    @pl.when(ki <= last_k)
    def _step():
        qpos = q0 + lax.broadcasted_iota(I32, (tq, tk), 0)
        kpos = ki * tk + lax.broadcasted_iota(I32, (tq, tk), 1)
        ok = kpos <= qpos
        for h in range(N_HEADS):
            p = h // 2
            s = _dot_nt(qs_ref[h], k_ref[0, :, p * LANES:(p + 1) * LANES])
            s = s + cq_ref[0, :, h:h + 1] - ck_ref[0, h:h + 1, :]
            s = jnp.where(ok, s, NEG)
            _softmax_step(h, s, v_ref[0, :, p * LANES:(p + 1) * LANES], m_ref, l_ref, acc_ref)

    @pl.when(ki == pl.num_programs(2) - 1)
    def _fin():
        _write_heads(o_ref, l_ref, acc_ref, lane)


def _fox_attention(q, k, v, cq, ck, past):
    B, T, _ = q.shape
    Tkp = k.shape[1]
    tq = _tile(T, 512)
    tk = _tile(Tkp, 512)

    def kblk(qi, ki):
        return jnp.minimum(ki, (past + qi * tq + tq - 1) // tk)

    kern = functools.partial(_fox_attn_kernel, tq=tq, tk=tk, past=past)
    return pl.pallas_call(
        kern, grid=(B, T // tq, Tkp // tk),
        in_specs=[pl.BlockSpec((1, tq, BRANCH_W), lambda b, qi, ki: (b, qi, 0)),
                  pl.BlockSpec((1, tk, BRANCH_W), lambda b, qi, ki: (b, kblk(qi, ki), 0)),
                  pl.BlockSpec((1, tk, BRANCH_W), lambda b, qi, ki: (b, kblk(qi, ki), 0)),
                  pl.BlockSpec((1, tq, N_HEADS), lambda b, qi, ki: (b, qi, 0)),
                  pl.BlockSpec((1, N_HEADS, tk), lambda b, qi, ki: (b, 0, kblk(qi, ki)))],
        out_specs=pl.BlockSpec((1, tq, BRANCH_W), lambda b, qi, ki: (b, qi, 0)),
        out_shape=jax.ShapeDtypeStruct((B, T, BRANCH_W), BF16),
        scratch_shapes=[pltpu.VMEM((N_HEADS, tq, LANES), BF16)] + _softmax_scratch(tq),
        compiler_params=_cparams("parallel", "parallel", "arbitrary"),
    )(q, k, v, cq, ck)


def _dsa_attn_kernel(q_ref, k_ref, v_ref, iq_ref, ikt_ref, iw_ref, o_ref,
                     key_ref, qs_ref, iqs_ref, m_ref, l_ref, acc_ref, cidx_ref,
                     *, tq, tk, past, n_keys, padded, n_keep, idx_bits):
    qi = pl.program_id(1)
    q0 = past + qi * tq
    adm_end = jnp.minimum(n_keys, ((q0 + tq - 1) // CHUNK + 1) * CHUNK)
    nkb = (adm_end + tk - 1) // tk
    lane = lax.broadcasted_iota(I32, (tq, LANES), 1)
    qpos = q0 + lax.broadcasted_iota(I32, (tq, tk), 0)
    col = lax.broadcasted_iota(I32, (tq, tk), 1)

    _init_softmax(m_ref, l_ref, acc_ref)
    for h in range(N_HEADS):
        q = q_ref[0, :, (h // 2) * LANES:(h // 2 + 1) * LANES]
        qs_ref[h] = _keep_head(q, lane, h, HEAD_DIM)
        iq = iq_ref[0, :, (h // 4) * LANES:(h // 4 + 1) * LANES]
        iqs_ref[h] = _keep_head(iq, lane, h, IDX_DIM)

    def score_body(kb, carry):
        ikt = ikt_ref[0, kb]
        sc = jnp.zeros((tq, tk), F32)
        for h in range(IDX_H):
            sc = sc + iw_ref[0, :, h:h + 1] * jnp.maximum(_dot(iqs_ref[h], ikt), 0.0)
        kpos = kb * tk + col
        bits = pltpu.bitcast(sc + 0.0, I32)
        okey = jnp.where(bits < 0, bits ^ 0x7FFFFFFF, bits)
        kchunk = jnp.right_shift(kpos, CHUNK_SHIFT)
        if padded:
            kchunk = jnp.where(kpos < n_keys, kchunk, BIG_IDX)
        adm = kchunk <= jnp.right_shift(qpos, CHUNK_SHIFT)
        key_ref[kb] = jnp.where(adm, okey, KEY_NEG_INF)
        return carry

    lax.fori_loop(0, nkb, score_body, 0)

    def count(pred):
        def body(kb, c):
            return c + jnp.sum(pred(key_ref[kb], kb), axis=1, keepdims=True)
        return lax.fori_loop(0, nkb, body, jnp.zeros((tq, 1), F32))

    def ones_where(cond):
        return jnp.where(cond, 1.0, 0.0)

    keep = float(n_keep)
    c_nonneg = count(lambda kt, kb: ones_where(kt >= 0))
    base = jnp.where(c_nonneg >= keep, jnp.zeros((tq, 1), I32), jnp.full((tq, 1), INT_MIN, I32))

    def bit_body(j, base):
        cand = base | jnp.left_shift(jnp.int32(1), 30 - j)
        c = count(lambda kt, kb: ones_where(kt >= cand))
        return jnp.where(c >= keep, cand, base)

    thr = lax.fori_loop(0, 31, bit_body, base)

    c_gt = count(lambda kt, kb: ones_where(kt > thr))
    c_ge = count(lambda kt, kb: ones_where(kt >= thr))
    need = keep - c_gt
    few = thr == KEY_NEG_INF
    tied = jnp.where(few, 0.0, ones_where(c_ge > keep))
    cidx_ref[...] = jnp.where(few, -1, BIG_IDX)

    @pl.when(jnp.max(tied) > 0.5)
    def _ties():
        def idx_body(j, b):
            cand = b | jnp.left_shift(jnp.int32(1), idx_bits - 1 - j)
            f = count(lambda kt, kb: jnp.where(kt == thr, ones_where(kb * tk + col < cand), 0.0))
            return jnp.where(f < need, cand, b)
        b = lax.fori_loop(0, idx_bits, idx_body, jnp.zeros((tq, 1), I32))
        cidx_ref[...] = jnp.where(few, -1, jnp.where(tied > 0.5, b, BIG_IDX))

    cidx = cidx_ref[...]

    def attn_body(kb, carry):
        k0 = pl.multiple_of(kb * tk, tk)
        kt = key_ref[kb]
        kpos = kb * tk + col
        bias = jnp.where(kt > thr, 0.0, jnp.where(kt == thr, jnp.where(kpos <= cidx, 0.0, NEG), NEG))
        for h in range(N_HEADS):
            p = h // 2
            s = _dot_nt(qs_ref[h], k_ref[0, pl.ds(k0, tk), p * LANES:(p + 1) * LANES]) + bias
            _softmax_step(h, s, v_ref[0, pl.ds(k0, tk), p * LANES:(p + 1) * LANES], m_ref, l_ref, acc_ref)
        return carry

    lax.fori_loop(0, nkb, attn_body, 0)
    _write_heads(o_ref, l_ref, acc_ref, lane)


def _dsa_attention(q, k, v, iq, ikt, iw, past, n_keys):
    B, T, _ = q.shape
    Tkp = k.shape[1]
    tq = _tile(T, 512)
    tk = _tile(Tkp, 512)
    nkb = Tkp // tk
    n_keep = min(TOPK_MAX, n_keys // 4)
    idx_bits = max(1, (Tkp - 1).bit_length())
    kern = functools.partial(_dsa_attn_kernel, tq=tq, tk=tk, past=past, n_keys=n_keys, padded=Tkp != n_keys,
                             n_keep=n_keep, idx_bits=idx_bits)
    return pl.pallas_call(
        kern, grid=(B, T // tq),
        in_specs=[pl.BlockSpec((1, tq, BRANCH_W), lambda b, qi: (b, qi, 0)),
                  pl.BlockSpec((1, Tkp, BRANCH_W), lambda b, qi: (b, 0, 0)),
                  pl.BlockSpec((1, Tkp, BRANCH_W), lambda b, qi: (b, 0, 0)),
                  pl.BlockSpec((1, tq, IDX_H * IDX_DIM), lambda b, qi: (b, qi, 0)),
                  pl.BlockSpec((1, nkb, LANES, tk), lambda b, qi: (b, 0, 0, 0)),
                  pl.BlockSpec((1, tq, IDX_H), lambda b, qi: (b, qi, 0))],
        out_specs=pl.BlockSpec((1, tq, BRANCH_W), lambda b, qi: (b, qi, 0)),
        out_shape=jax.ShapeDtypeStruct((B, T, BRANCH_W), BF16),
        scratch_shapes=[pltpu.VMEM((nkb, tq, tk), I32),
                        pltpu.VMEM((N_HEADS, tq, LANES), BF16),
                        pltpu.VMEM((IDX_H, tq, LANES), BF16)] + _softmax_scratch(tq)
                       + [pltpu.VMEM((tq, 1), I32)],
        compiler_params=_cparams("parallel", "arbitrary"),
    )(q, k, v, iq, ikt, iw)


def _heads_t(x_ref, qt_ref, lane, width, row0=0):
    per = LANES // width
    for h in range(N_HEADS):
        s = h // per
        x = x_ref[0, :, s * LANES:(s + 1) * LANES].astype(F32)
        x = jnp.where(_own_lanes(lane, h, width), x, 0.0)
        qt_ref[h, row0:row0 + LANES, :] = x.T.astype(BF16)


def _softmax_step_t(h, s, vt_slab, m_ref, l_ref, acc_ref):
    m_prev = m_ref[h]
    m_new = jnp.maximum(m_prev, jnp.max(s, axis=0, keepdims=True))
    alpha = jnp.exp(m_prev - m_new)
    p = jnp.exp(s - m_new)
    l_ref[h] = alpha * l_ref[h] + jnp.sum(p, axis=0, keepdims=True)
    acc_ref[h] = alpha * acc_ref[h] + _dot(vt_slab, p.astype(BF16))
    m_ref[h] = m_new


def _write_heads_t(o_ref, l_ref, acc_ref, tq):
    row = lax.broadcasted_iota(I32, (LANES, tq), 0)
    for p in range(N_HEADS // 2):
        a = acc_ref[2 * p] * (1.0 / l_ref[2 * p])
        b = acc_ref[2 * p + 1] * (1.0 / l_ref[2 * p + 1])
        o_ref[0, :, p * LANES:(p + 1) * LANES] = jnp.where(row < HEAD_DIM, a, b).T.astype(BF16)


def _softmax_scratch_t(tq):
    return [pltpu.VMEM((N_HEADS, 1, tq), F32), pltpu.VMEM((N_HEADS, 1, tq), F32),
            pltpu.VMEM((N_HEADS, LANES, tq), F32)]


def _mla_attn_t_kernel(qn_ref, qr_ref, kpp_ref, vt_ref, o_ref, qt_ref, m_ref, l_ref, acc_ref,
                       *, tq, tk, past, n_keys):
    qi = pl.program_id(1)
    ki = pl.program_id(2)
    q0 = past + qi * tq
    last_k = (jnp.minimum(n_keys, ((q0 + tq - 1) // CHUNK + 1) * CHUNK) - 1) // tk

    @pl.when(ki == 0)
    def _init():
        _init_softmax(m_ref, l_ref, acc_ref)
        lane = lax.broadcasted_iota(I32, (tq, LANES), 1)
        _heads_t(qn_ref, qt_ref, lane, HEAD_DIM)
        _heads_t(qr_ref, qt_ref, lane, ROPE_A, row0=LANES)

    @pl.when(ki <= last_k)
    def _step():
        kpos = ki * tk + lax.broadcasted_iota(I32, (tk, tq), 0)
        qpos = q0 + lax.broadcasted_iota(I32, (tk, tq), 1)
        ok = jnp.right_shift(kpos, CHUNK_SHIFT) <= jnp.right_shift(qpos, CHUNK_SHIFT)
        scale = (HEAD_DIM + ROPE_A) ** -0.5
        for h in range(N_HEADS):
            p = h // 2
            s = _dot(kpp_ref[0, p], qt_ref[h]) * scale
            s = jnp.where(ok, s, NEG)
            _softmax_step_t(h, s, vt_ref[0, 0, p * LANES:(p + 1) * LANES, :], m_ref, l_ref, acc_ref)

    @pl.when(ki == pl.num_programs(2) - 1)
    def _fin():
        _write_heads_t(o_ref, l_ref, acc_ref, tq)


def _fox_attn_t_kernel(q_ref, k_ref, vt_ref, cq_ref, ck_ref, o_ref, qt_ref, m_ref, l_ref, acc_ref,
                       *, tq, tk, past):
    qi = pl.program_id(1)
    ki = pl.program_id(2)
    q0 = past + qi * tq
    last_k = (q0 + tq - 1) // tk

    @pl.when(ki == 0)
    def _init():
        _init_softmax(m_ref, l_ref, acc_ref)
        _heads_t(q_ref, qt_ref, lax.broadcasted_iota(I32, (tq, LANES), 1), HEAD_DIM)

    @pl.when(ki <= last_k)
    def _step():
        kpos = ki * tk + lax.broadcasted_iota(I32, (tk, tq), 0)
        qpos = q0 + lax.broadcasted_iota(I32, (tk, tq), 1)
        ok = kpos <= qpos
        for h in range(N_HEADS):
            p = h // 2
            s = _dot(k_ref[0, :, p * LANES:(p + 1) * LANES], qt_ref[h])
            s = s + cq_ref[0, h:h + 1, :] - ck_ref[0, :, h:h + 1]
            s = jnp.where(ok, s, NEG)
            _softmax_step_t(h, s, vt_ref[0, 0, p * LANES:(p + 1) * LANES, :], m_ref, l_ref, acc_ref)

    @pl.when(ki == pl.num_programs(2) - 1)
    def _fin():
        _write_heads_t(o_ref, l_ref, acc_ref, tq)


def _blocked_t(v, tk):
    B, Tk, C = v.shape
    return jnp.swapaxes(v.reshape(B, Tk // tk, tk, C), 2, 3)


def _mla_attention_t(qn, qr, kpp, v, past, n_keys):
    B, T, _ = qn.shape
    Tk = v.shape[1]
    tq = _tile(T, 512)
    tk = _tile(Tk, 512)

    def kblk(qi, ki):
        last = (jnp.minimum(n_keys, ((past + qi * tq + tq - 1) // CHUNK + 1) * CHUNK) - 1) // tk
        return jnp.minimum(ki, last)

    kern = functools.partial(_mla_attn_t_kernel, tq=tq, tk=tk, past=past, n_keys=n_keys)
    return pl.pallas_call(
        kern, grid=(B, T // tq, Tk // tk),
        in_specs=[pl.BlockSpec((1, tq, BRANCH_W), lambda b, qi, ki: (b, qi, 0)),
                  pl.BlockSpec((1, tq, 2 * LANES), lambda b, qi, ki: (b, qi, 0)),
                  pl.BlockSpec((1, N_HEADS // 2, tk, 2 * LANES), lambda b, qi, ki: (b, 0, kblk(qi, ki), 0)),
                  pl.BlockSpec((1, 1, BRANCH_W, tk), lambda b, qi, ki: (b, kblk(qi, ki), 0, 0))],
        out_specs=pl.BlockSpec((1, tq, BRANCH_W), lambda b, qi, ki: (b, qi, 0)),
        out_shape=jax.ShapeDtypeStruct((B, T, BRANCH_W), BF16),
        scratch_shapes=[pltpu.VMEM((N_HEADS, 2 * LANES, tq), BF16)] + _softmax_scratch_t(tq),
        compiler_params=_cparams("parallel", "parallel", "arbitrary"),
    )(qn, qr, kpp, _blocked_t(v, tk))


def _fox_attention_t(q, k, v, cq_t, ck, past):
    B, T, _ = q.shape
    Tk = k.shape[1]
    tq = _tile(T, 512)
    tk = _tile(Tk, 512)

    def kblk(qi, ki):
        return jnp.minimum(ki, (past + qi * tq + tq - 1) // tk)

    kern = functools.partial(_fox_attn_t_kernel, tq=tq, tk=tk, past=past)
    return pl.pallas_call(
        kern, grid=(B, T // tq, Tk // tk),
        in_specs=[pl.BlockSpec((1, tq, BRANCH_W), lambda b, qi, ki: (b, qi, 0)),
                  pl.BlockSpec((1, tk, BRANCH_W), lambda b, qi, ki: (b, kblk(qi, ki), 0)),
                  pl.BlockSpec((1, 1, BRANCH_W, tk), lambda b, qi, ki: (b, kblk(qi, ki), 0, 0)),
                  pl.BlockSpec((1, N_HEADS, tq), lambda b, qi, ki: (b, 0, qi)),
                  pl.BlockSpec((1, tk, N_HEADS), lambda b, qi, ki: (b, kblk(qi, ki), 0))],
        out_specs=pl.BlockSpec((1, tq, BRANCH_W), lambda b, qi, ki: (b, qi, 0)),
        out_shape=jax.ShapeDtypeStruct((B, T, BRANCH_W), BF16),
        scratch_shapes=[pltpu.VMEM((N_HEADS, LANES, tq), BF16)] + _softmax_scratch_t(tq),
        compiler_params=_cparams("parallel", "parallel", "arbitrary"),
    )(q, k, _blocked_t(v, tk), cq_t, ck)


def _dsa_attn_t_kernel(q_ref, k_ref, vt_ref, iq_ref, ik_ref, iw_ref, o_ref,
                       key_ref, qt_ref, iqt_ref, m_ref, l_ref, acc_ref, cidx_ref,
                       *, tq, tk, past, n_keys, n_keep, idx_bits):
    qi = pl.program_id(1)
    q0 = past + qi * tq
    adm_end = jnp.minimum(n_keys, ((q0 + tq - 1) // CHUNK + 1) * CHUNK)
    nkb = (adm_end + tk - 1) // tk
    lane = lax.broadcasted_iota(I32, (tq, LANES), 1)
    qchunk = jnp.right_shift(q0 + lax.broadcasted_iota(I32, (tk, tq), 1), CHUNK_SHIFT)
    krow = lax.broadcasted_iota(I32, (tk, tq), 0)

    _init_softmax(m_ref, l_ref, acc_ref)
    _heads_t(q_ref, qt_ref, lane, HEAD_DIM)
    _heads_t(iq_ref, iqt_ref, lane, IDX_DIM)

    def score_body(kb, carry):
        k0 = pl.multiple_of(kb * tk, tk)
        ik = ik_ref[0, pl.ds(k0, tk), :]
        sc = jnp.zeros((tk, tq), F32)
        for h in range(IDX_H):
            sc = sc + iw_ref[0, h:h + 1, :] * jnp.maximum(_dot(ik, iqt_ref[h]), 0.0)
        bits = pltpu.bitcast(sc + 0.0, I32)
        okey = jnp.where(bits < 0, bits ^ 0x7FFFFFFF, bits)
        adm = jnp.right_shift(k0 + krow, CHUNK_SHIFT) <= qchunk
        key_ref[kb] = jnp.where(adm, okey, KEY_NEG_INF)
        return carry

    lax.fori_loop(0, nkb, score_body, 0)

    def count(pred):
        def body(kb, c):
            return c + jnp.sum(pred(key_ref[kb], kb), axis=0, keepdims=True)
        return lax.fori_loop(0, nkb, body, jnp.zeros((1, tq), F32))

    def ones_where(cond):
        return jnp.where(cond, 1.0, 0.0)

    keep = float(n_keep)
    c_nonneg = count(lambda kt, kb: ones_where(kt >= 0))
    base = jnp.where(c_nonneg >= keep, jnp.zeros((1, tq), I32), jnp.full((1, tq), INT_MIN, I32))

    def bit_body(j, base):
        cand = base | jnp.left_shift(jnp.int32(1), 30 - j)
        c = count(lambda kt, kb: ones_where(kt >= cand))
        return jnp.where(c >= keep, cand, base)

    thr = lax.fori_loop(0, 31, bit_body, base)

    c_gt = count(lambda kt, kb: ones_where(kt > thr))
    c_ge = count(lambda kt, kb: ones_where(kt >= thr))
    need = keep - c_gt
    few = thr == KEY_NEG_INF
    tied = jnp.where(few, 0.0, ones_where(c_ge > keep))
    cidx_ref[...] = jnp.where(few, -1, BIG_IDX)

    @pl.when(jnp.max(tied) > 0.5)
    def _ties():
        def idx_body(j, b):
            cand = b | jnp.left_shift(jnp.int32(1), idx_bits - 1 - j)
            f = count(lambda kt, kb: jnp.where(kt == thr, ones_where(kb * tk + krow < cand), 0.0))
            return jnp.where(f < need, cand, b)
        b = lax.fori_loop(0, idx_bits, idx_body, jnp.zeros((1, tq), I32))
        cidx_ref[...] = jnp.where(few, -1, jnp.where(tied > 0.5, b, BIG_IDX))

    cidx = cidx_ref[...]

    def attn_body(kb, carry):
        k0 = pl.multiple_of(kb * tk, tk)
        kt = key_ref[kb]
        kpos = k0 + krow
        bias = jnp.where(kt > thr, 0.0, jnp.where(kt == thr, jnp.where(kpos <= cidx, 0.0, NEG), NEG))
        for h in range(N_HEADS):
            p = h // 2
            s = _dot(k_ref[0, pl.ds(k0, tk), p * LANES:(p + 1) * LANES], qt_ref[h]) + bias
            _softmax_step_t(h, s, vt_ref[0, kb, p * LANES:(p + 1) * LANES, :], m_ref, l_ref, acc_ref)
        return carry

    lax.fori_loop(0, nkb, attn_body, 0)
    _write_heads_t(o_ref, l_ref, acc_ref, tq)


def _dsa_attention_t(q, k, v, iq, ik4, iw_t, past, n_keys):
    B, T, _ = q.shape
    Tk = k.shape[1]
    tq = _tile(T, 512)
    tk = _tile(Tk, 512)
    nkb = Tk // tk
    n_keep = min(TOPK_MAX, n_keys // 4)
    idx_bits = max(1, (Tk - 1).bit_length())
    kern = functools.partial(_dsa_attn_t_kernel, tq=tq, tk=tk, past=past, n_keys=n_keys,
                             n_keep=n_keep, idx_bits=idx_bits)
    return pl.pallas_call(
        kern, grid=(B, T // tq),
        in_specs=[pl.BlockSpec((1, tq, BRANCH_W), lambda b, qi: (b, qi, 0)),
                  pl.BlockSpec((1, Tk, BRANCH_W), lambda b, qi: (b, 0, 0)),
                  pl.BlockSpec((1, nkb, BRANCH_W, tk), lambda b, qi: (b, 0, 0, 0)),
                  pl.BlockSpec((1, tq, IDX_H * IDX_DIM), lambda b, qi: (b, qi, 0)),
                  pl.BlockSpec((1, Tk, LANES), lambda b, qi: (b, 0, 0)),
                  pl.BlockSpec((1, IDX_H, tq), lambda b, qi: (b, 0, qi))],
        out_specs=pl.BlockSpec((1, tq, BRANCH_W), lambda b, qi: (b, qi, 0)),
        out_shape=jax.ShapeDtypeStruct((B, T, BRANCH_W), BF16),
        scratch_shapes=[pltpu.VMEM((nkb, tk, tq), I32),
                        pltpu.VMEM((N_HEADS, LANES, tq), BF16),
                        pltpu.VMEM((IDX_H, LANES, tq), BF16)] + _softmax_scratch_t(tq)
                       + [pltpu.VMEM((1, tq), I32)],
        compiler_params=_cparams("parallel", "arbitrary"),
    )(q, k, _blocked_t(v, tk), iq, ik4, iw_t)


def _merge_kernel(h_ref, hb_ref, oa_ref, ob_ref, oc_ref, wg_ref, bg_ref, wbr_ref, wout_ref, g_ref, b_ref,
                  wr_ref, br_ref, x_ref, xb_ref, idx_ref, gate_ref):
    hb = hb_ref[...]
    d = D_MODEL
    merged = None
    for n, o_ref in enumerate((oa_ref, ob_ref, oc_ref)):
        gt = _dot(hb, wg_ref[:, n * d:(n + 1) * d]) + bg_ref[:, n * d:(n + 1) * d]
        term = (1.0 / (1.0 + jnp.exp(-gt))) * _dot(o_ref[...], wbr_ref[n])
        merged = term if merged is None else merged + term
    out = _dot(merged.astype(BF16), wout_ref[...])
    x = _layer_norm(DN_ALPHA * h_ref[...] + out, g_ref[...], b_ref[...])
    xb = x.astype(BF16)
    x_ref[...] = x
    xb_ref[...] = xb

    logits = _dot(xb, wr_ref[...]) + br_ref[...]
    tm = logits.shape[0]
    lane = lax.broadcasted_iota(I32, (tm, N_EXPERTS), 1).astype(F32)
    k4 = lax.broadcasted_iota(I32, (tm, TOP_K), 1)
    vals = jnp.zeros((tm, TOP_K), F32)
    idxs = jnp.zeros((tm, TOP_K), F32)
    for k in range(TOP_K):
        mx = jnp.max(logits, axis=1, keepdims=True)
        ix = jnp.min(jnp.where(logits == mx, lane, float(N_EXPERTS)), axis=1, keepdims=True)
        vals = jnp.where(k4 == k, mx, vals)
        idxs = jnp.where(k4 == k, ix, idxs)
        logits = jnp.where(lane == ix, -jnp.inf, logits)
    e = jnp.exp(vals - jnp.max(vals, axis=1, keepdims=True))
    idx_ref[...] = idxs.astype(I32)
    gate_ref[...] = e * (1.0 / jnp.sum(e, axis=1, keepdims=True))


def _merge(h, hb, oa, ob, oc, wg, bg, wbr, wout, g, b, wr, br):
    n, d = h.shape
    tm = _tile(n, 256)
    row = lambda w: pl.BlockSpec((tm, w), lambda i: (i, 0))
    return pl.pallas_call(
        _merge_kernel, grid=(n // tm,),
        in_specs=[row(d), row(d), row(BRANCH_W), row(BRANCH_W), row(BRANCH_W),
                  _full(wg.shape), _full(bg.shape), _full(wbr.shape), _full(wout.shape),
                  _full(g.shape), _full(b.shape), _full(wr.shape), _full(br.shape)],
        out_specs=[row(d), row(d), row(TOP_K), row(TOP_K)],
        out_shape=[jax.ShapeDtypeStruct((n, d), F32), jax.ShapeDtypeStruct((n, d), BF16),
                   jax.ShapeDtypeStruct((n, TOP_K), I32), jax.ShapeDtypeStruct((n, TOP_K), F32)],
        compiler_params=_cparams("parallel"),
    )(h, hb, oa, ob, oc, wg, bg, wbr, wout, g, b, wr, br)


def _expert_kernel(blk_e_ref, idx_hbm, x_hbm, w1_ref, b1_ref, w2_ref, b2_ref, y_hbm,
                   idx_smem, xbuf, ybuf, isem, gsem, ssem, *, bs, n_blocks):
    del blk_e_ref
    i = pl.program_id(0)
    slot = i % 2
    nslot = 1 - slot

    def idx_copy(row, s):
        return pltpu.make_async_copy(idx_hbm.at[row], idx_smem.at[s], isem.at[s])

    def start_gather(s):
        for r in range(bs):
            tok = idx_smem[s, r]
            pltpu.make_async_copy(x_hbm.at[pl.ds(tok, 1)], xbuf.at[s, pl.ds(r, 1)], gsem.at[s]).start()

    def wait_gather(s):
        pltpu.make_async_copy(x_hbm.at[pl.ds(0, bs)], xbuf.at[s], gsem.at[s]).wait()

    def start_scatter(s_buf, s_idx):
        for r in range(bs):
            dst = idx_smem[s_idx, bs + r]
            pltpu.make_async_copy(ybuf.at[s_buf, pl.ds(r, 1)], y_hbm.at[pl.ds(dst, 1)], ssem.at[s_buf]).start()

    def wait_scatter(s):
        pltpu.make_async_copy(ybuf.at[s], y_hbm.at[pl.ds(0, bs)], ssem.at[s]).wait()

    def step(gather_next, scatter_prev):
        idx_copy(i + 2, slot).start()
        idx_copy(i + 1, nslot).wait()
        wait_gather(slot)

        @pl.when(i >= 2)
        def _():
            wait_scatter(slot)

        if gather_next:
            start_gather(nslot)
        if scatter_prev:
            start_scatter(nslot, nslot)
        x = xbuf[slot].astype(BF16)
        gu = _dot(x, w1_ref[0]) + b1_ref[0]
        g = jnp.minimum(gu[:, :D_FF], SWIGLU_LIMIT)
        u = jnp.clip(gu[:, D_FF:], -SWIGLU_LIMIT, SWIGLU_LIMIT)
        a = g * (1.0 / (1.0 + jnp.exp(-SWIGLU_ALPHA * g))) * (u + 1.0)
        ybuf[slot] = _dot(a.astype(BF16), w2_ref[0]) + b2_ref[0]

    @pl.when(i == 0)
    def _first():
        idx_copy(0, 0).start()
        idx_copy(0, 0).wait()
        start_gather(0)
        idx_copy(1, 1).start()
        step(True, False)

    @pl.when(jnp.logical_and(i > 0, i < n_blocks - 1))
    def _middle():
        step(True, True)

    @pl.when(i == n_blocks - 1)
    def _last():
        step(False, True)
        idx_copy(i + 2, slot).wait()
        start_scatter(slot, slot)
        wait_scatter(nslot)
        wait_scatter(slot)


def _experts(x, blk_e, idx, w1, b1, w2, b2, n_rows_out, bs):
    n, d = x.shape
    n_blocks = idx.shape[0] - 2
    assert n_blocks >= 2
    kern = functools.partial(_expert_kernel, bs=bs, n_blocks=n_blocks)
    grid_spec = pltpu.PrefetchScalarGridSpec(
        num_scalar_prefetch=1, grid=(n_blocks,),
        in_specs=[pl.BlockSpec(memory_space=pl.ANY), pl.BlockSpec(memory_space=pl.ANY),
                  pl.BlockSpec((1, d, 2 * D_FF), lambda i, be: (be[i], 0, 0)),
                  pl.BlockSpec((1, 1, 2 * D_FF), lambda i, be: (be[i], 0, 0)),
                  pl.BlockSpec((1, D_FF, d), lambda i, be: (be[i], 0, 0)),
                  pl.BlockSpec((1, 1, d), lambda i, be: (be[i], 0, 0))],
        out_specs=pl.BlockSpec(memory_space=pl.ANY),
        scratch_shapes=[pltpu.SMEM((2, 2 * bs), I32),
                        pltpu.VMEM((2, bs, d), F32), pltpu.VMEM((2, bs, d), F32),
                        pltpu.SemaphoreType.DMA((2,)), pltpu.SemaphoreType.DMA((2,)),
                        pltpu.SemaphoreType.DMA((2,))])
    return pl.pallas_call(
        kern, grid_spec=grid_spec,
        out_shape=jax.ShapeDtypeStruct((n_rows_out, d), F32),
        compiler_params=_cparams("arbitrary"),
    )(blk_e, idx, x, w1, b1, w2, b2)


def _dispatch_tables(top_idx, bs):
    n = top_idx.shape[0]
    m = n * TOP_K
    flat_e = top_idx.reshape(-1)
    order = jnp.argsort(flat_e).astype(I32)
    counts = jnp.bincount(flat_e, length=N_EXPERTS).astype(I32)
    padded = (counts + bs - 1) // bs * bs
    start = jnp.cumsum(counts) - counts
    pend = jnp.cumsum(padded)
    pstart = pend - padded
    n_blocks = -(-m // bs) + N_EXPERTS
    cap = n_blocks * bs
    blk_e = jnp.minimum(jnp.searchsorted(pend, jnp.arange(n_blocks, dtype=I32) * bs, side='right'),
                        N_EXPERTS - 1).astype(I32)
    slot = jnp.arange(cap, dtype=I32)
    e = blk_e[slot // bs]
    off = slot - pstart[e]
    is_pad = off >= counts[e]
    flat = order[jnp.clip(start[e] + off, 0, m - 1)]
    src = jnp.where(is_pad, 0, flat // TOP_K)
    dst = jnp.where(is_pad, m - 1 + jnp.cumsum(is_pad.astype(I32)), (flat % TOP_K) * n + flat // TOP_K)
    zeros = jnp.zeros((2, bs), I32)
    idx = jnp.concatenate([jnp.concatenate([src.reshape(n_blocks, bs), zeros], axis=0),
                           jnp.concatenate([zeros, dst.reshape(n_blocks, bs)], axis=0)], axis=1)
    return blk_e, idx, cap


def _combine_kernel(x_ref, y0_ref, y1_ref, y2_ref, y3_ref, gate_ref, g_ref, b_ref, o_ref, ob_ref):
    y = gate_ref[:, 0:1] * y0_ref[...]
    for k, y_ref in enumerate((y1_ref, y2_ref, y3_ref), start=1):
        y = y + gate_ref[:, k:k + 1] * y_ref[...]
    out = _layer_norm(DN_ALPHA * x_ref[...] + y, g_ref[...], b_ref[...])
    o_ref[...] = out
    ob_ref[...] = out.astype(BF16)


def _combine(x, y, gate, g, b):
    n, d = x.shape
    tm = _tile(n, 256)
    nt = n // tm
    row = lambda w: pl.BlockSpec((tm, w), lambda i: (i, 0))
    y_specs = [pl.BlockSpec((tm, d), lambda i, k=k: (k * nt + i, 0)) for k in range(TOP_K)]
    return pl.pallas_call(
        _combine_kernel, grid=(nt,),
        in_specs=[row(d)] + y_specs + [row(TOP_K), _full((1, d)), _full((1, d))],
        out_specs=[row(d), row(d)],
        out_shape=[jax.ShapeDtypeStruct((n, d), F32), jax.ShapeDtypeStruct((n, d), BF16)],
        compiler_params=_cparams("parallel"),
    )(x, y, y, y, y, gate, g, b)


def _deinterleave_kernel(w_ref, sel_ref, o_ref):
    half = o_ref.shape[2] // 2
    for s in range(half // LANES):
        y = _dot(w_ref[0, :, 2 * s * LANES:2 * (s + 1) * LANES].astype(BF16), sel_ref[...])
        o_ref[0, :, s * LANES:(s + 1) * LANES] = y[:, :LANES].astype(BF16)
        o_ref[0, :, half + s * LANES:half + (s + 1) * LANES] = y[:, LANES:].astype(BF16)


def _deinterleave_up(w_up):
    L, E, d, f2 = w_up.shape
    j = jnp.arange(2 * LANES)
    sel = (j[:, None] == jnp.where(j < LANES, 2 * j, 2 * (j - LANES) + 1)[None, :]).astype(BF16)
    out = pl.pallas_call(
        _deinterleave_kernel, grid=(L * E,),
        in_specs=[pl.BlockSpec((1, d, f2), lambda i: (i, 0, 0)), _full((2 * LANES, 2 * LANES))],
        out_specs=pl.BlockSpec((1, d, f2), lambda i: (i, 0, 0)),
        out_shape=jax.ShapeDtypeStruct((L * E, d, f2), BF16),
        compiler_params=_cparams("parallel"),
    )(w_up.reshape(L * E, d, f2), sel)
    return out.reshape(L, E, d, f2)


def _prep_layer(l, w_in, b_f, b_gate, g_qa, g_kva, w_uq, w_ukv, w_br, w_out, ln1_g, ln1_b,
                w_router, b_router, w_up, b_up, w_down, b_down, ln2_g, ln2_b):
    w = w_in[l]
    o = [0]

    def take(width):
        s = w[:, o[0]:o[0] + width]
        o[0] += width
        return s

    def pad_to(a, width):
        return jnp.pad(a, ((0, 0), (0, width - a.shape[1])))

    qa, kva, kra = take(Q_LORA), take(KV_LORA), take(ROPE_A)
    qb, kb, vb, fb = take(BRANCH_W), take(BRANCH_W), take(BRANCH_W), take(N_HEADS)
    qc, kc, vc = take(BRANCH_W), take(BRANCH_W), take(BRANCH_W)
    iqc, ikc, iwc = take(IDX_H * IDX_DIM), take(IDX_DIM), take(IDX_H)
    gt = take(N_BRANCH * D_MODEL)
    p = {}
    p['w_mla'] = jnp.concatenate([qa, kva, jnp.tile(kra, (1, 4))], axis=1).astype(BF16)
    p['w_fox'] = jnp.concatenate([qb, kb, vb, pad_to(fb, LANES)], axis=1).astype(BF16)
    p['w_dsa'] = jnp.concatenate([qc, kc, vc, iqc, jnp.tile(ikc, (1, 4)), pad_to(iwc, LANES)], axis=1).astype(BF16)
    p['w_gate'] = gt.astype(BF16)
    uq = w_uq[l].reshape(Q_LORA, N_HEADS, HEAD_DIM + ROPE_A)
    p['w_uq'] = jnp.concatenate([uq[:, :, :HEAD_DIM].reshape(Q_LORA, -1),
                                 uq[:, :, HEAD_DIM:].reshape(Q_LORA, -1)], axis=1).astype(BF16)
    ukv = w_ukv[l].reshape(KV_LORA, N_HEADS, 2 * HEAD_DIM)
    p['w_ukv'] = jnp.concatenate([ukv[:, :, :HEAD_DIM].reshape(KV_LORA, -1),
                                  ukv[:, :, HEAD_DIM:].reshape(KV_LORA, -1)], axis=1).astype(BF16)
    p['b_f'] = b_f[l].reshape(1, N_HEADS)
    p['b_gate'] = b_gate[l].reshape(1, N_BRANCH * D_MODEL)
    p['g_qa'] = g_qa[l].reshape(1, Q_LORA)
    p['g_kva'] = g_kva[l].reshape(1, KV_LORA)
    p['w_br'] = w_br[l].astype(BF16)
    p['w_out'] = w_out[l].astype(BF16)
    p['ln1'] = (ln1_g[l].reshape(1, -1), ln1_b[l].reshape(1, -1))
    p['ln2'] = (ln2_g[l].reshape(1, -1), ln2_b[l].reshape(1, -1))
    p['w_router'] = w_router[l].astype(BF16)
    p['b_router'] = b_router[l].reshape(1, N_EXPERTS)
    p['w_up'] = w_up[l]
    p['b_up'] = jnp.concatenate([b_up[l][:, 0::2], b_up[l][:, 1::2]], axis=1)[:, None, :]
    p['w_down'] = w_down[l].astype(BF16)
    p['b_down'] = b_down[l][:, None, :]
    return p


def _pad_time(a, t_pad):
    return a if a.shape[1] == t_pad else jnp.pad(a, ((0, 0), (0, t_pad - a.shape[1])) + ((0, 0),) * (a.ndim - 2))


def _trunk(x, cache, params, ln_in, moe_block):
    B, T, D = x.shape
    n = B * T
    past = 0 if cache is None else cache[0].shape[2]
    n_keys = past + T
    tkp = -(-n_keys // LANES) * LANES
    pos = jnp.arange(past, past + T, dtype=I32)
    tab_a = _rope_tables(pos, ROPE_A, ROPE_A)
    tab_c = _rope_tables(pos, ROT_C, HEAD_DIM)
    tab_i = _rope_tables(pos, ROT_IDX, IDX_DIM)
    tile_mat = (jnp.arange(ROPE_A)[:, None] == (jnp.arange(LANES)[None, :] % ROPE_A)).astype(BF16)
    tm = _tile(T, 512)
    key_major = T % LANES == 0 and tkp == n_keys

    h, hb = _ln_in(x.reshape(n, D), *ln_in)
    rows = [[] for _ in range(8)]
    for l, p in enumerate(params):
        hb3 = hb.reshape(B, T, D)
        qn, qr, ckv_new, kr_new = _proj_call(
            _mla_proj_kernel, hb3, [p['w_mla'], p['w_uq'], p['g_qa'], p['g_kva']], tab_a,
            [(BRANCH_W, BF16), (2 * LANES, BF16), (KV_LORA, F32), (ROPE_A, F32)], tm)
        qb, kb_new, vb_new, kb16, vb16, lf_new = _proj_call(
            _fox_proj_kernel, hb3, [p['w_fox'], p['b_f']], (),
            [(BRANCH_W, BF16), (BRANCH_W, F32), (BRANCH_W, F32), (BRANCH_W, BF16), (BRANCH_W, BF16),
             (N_HEADS, F32)], tm)
        qc, kc_new, vc_new, kc16, vc16, iq, ik_new, iw = _proj_call(
            _dsa_proj_kernel, hb3, [p['w_dsa']], tab_c + tab_i,
            [(BRANCH_W, BF16), (BRANCH_W, F32), (BRANCH_W, F32), (BRANCH_W, BF16), (BRANCH_W, BF16),
             (IDX_H * IDX_DIM, BF16), (IDX_DIM, F32), (IDX_H, F32)], tm)

        if cache is None:
            ckv_all, kr_all, lf_all, ik_all = ckv_new, kr_new, lf_new, ik_new
            kb_all, vb_all, kc_all, vc_all = kb16, vb16, kc16, vc16
        else:
            c_ckv, c_kr, c_kb, c_vb, c_lf, c_kc, c_vc, c_ik = (c[l] for c in cache)
            flat = lambda c: c.reshape(B, past, BRANCH_W).astype(BF16)
            ckv_all = jnp.concatenate([c_ckv, ckv_new], axis=1)
            kr_all = jnp.concatenate([c_kr, kr_new], axis=1)
            lf_all = jnp.concatenate([c_lf, lf_new], axis=1)
            ik_all = jnp.concatenate([c_ik, ik_new], axis=1)
            kb_all = jnp.concatenate([flat(c_kb), kb16], axis=1)
            vb_all = jnp.concatenate([flat(c_vb), vb16], axis=1)
            kc_all = jnp.concatenate([flat(c_kc), kc16], axis=1)
            vc_all = jnp.concatenate([flat(c_vc), vc16], axis=1)
        ckv_all, kr_all, lf_all, ik_all, kb_all, vb_all, kc_all, vc_all = (
            _pad_time(a, tkp) for a in (ckv_all, kr_all, lf_all, ik_all, kb_all, vb_all, kc_all, vc_all))

        kpp, v_a = _mla_kv(ckv_all, kr_all, p['w_ukv'], tile_mat)
        cum = _cumsum_time(jnp.swapaxes(lf_all, 1, 2))
        if key_major:
            o_a = _mla_attention_t(qn, qr, kpp, v_a, past, n_keys)
            o_b = _fox_attention_t(qb, kb_all, vb_all, cum[:, :, past:past + T], jnp.swapaxes(cum, 1, 2), past)
            ik4 = jnp.tile(ik_all.astype(BF16), (1, 1, LANES // IDX_DIM))
            o_c = _dsa_attention_t(qc, kc_all, vc_all, iq, ik4, jnp.swapaxes(iw, 1, 2), past, n_keys)
        else:
            o_a = _mla_attention(qn, qr, kpp, v_a, past, n_keys)
            cq = jnp.swapaxes(cum[:, :, past:past + T], 1, 2)
            o_b = _fox_attention(qb, kb_all, vb_all, cq, cum, past)
            tk = _tile(tkp, 512)
            ikt = jnp.swapaxes(ik_all.astype(BF16), 1, 2)
            ikt = jnp.tile(ikt, (1, LANES // IDX_DIM, 1))
            ikt = jnp.swapaxes(ikt.reshape(B, LANES, tkp // tk, tk), 1, 2)
            o_c = _dsa_attention(qc, kc_all, vc_all, iq, ikt, iw, past, n_keys)

        x1, x1b, top_idx, gate = _merge(
            h, hb, o_a.reshape(n, -1), o_b.reshape(n, -1), o_c.reshape(n, -1),
            p['w_gate'], p['b_gate'], p['w_br'], p['w_out'], *p['ln1'], p['w_router'], p['b_router'])

        blk_e, idx, n_rows = _dispatch_tables(top_idx, moe_block)
        y4 = _experts(x1, blk_e, idx, p['w_up'], p['b_up'], p['w_down'], p['b_down'], n_rows, moe_block)
        h, hb = _combine(x1, y4, gate, *p['ln2'])

        new = (ckv_new, kr_new, kb_new.reshape(B, T, N_HEADS, HEAD_DIM), vb_new.reshape(B, T, N_HEADS, HEAD_DIM),
               lf_new, kc_new.reshape(B, T, N_HEADS, HEAD_DIM), vc_new.reshape(B, T, N_HEADS, HEAD_DIM), ik_new)
        for acc, r in zip(rows, new):
            acc.append(r)
    return h.reshape(B, T, D), [jnp.stack(a) for a in rows]


def kernel(x_prompt, x_sample, cache_mla_ckv, cache_mla_krope, cache_fox_k, cache_fox_v, cache_fox_logf,
           cache_dsa_k, cache_dsa_v, cache_dsa_idxk, ln_in_g, ln_in_b, w_in, b_f, b_gate, g_qa, g_kva,
           w_uq, w_ukv, w_br, w_out, ln1_g, ln1_b, w_router, b_router, w_up, b_up, w_down, b_down,
           ln2_g, ln2_b):
    depth = w_in.shape[0]
    w_up = _deinterleave_up(w_up)
    params = [_prep_layer(l, w_in, b_f, b_gate, g_qa, g_kva, w_uq, w_ukv, w_br, w_out, ln1_g, ln1_b,
                          w_router, b_router, w_up, b_up, w_down, b_down, ln2_g, ln2_b) for l in range(depth)]
    ln_in = (ln_in_g, ln_in_b)
    y_p, st_p = _trunk(x_prompt, None, params, ln_in, moe_block=512)
    caches = (cache_mla_ckv, cache_mla_krope, cache_fox_k, cache_fox_v, cache_fox_logf,
              cache_dsa_k, cache_dsa_v, cache_dsa_idxk)
    y_s, st_s = _trunk(x_sample, caches, params, ln_in, moe_block=128)
    out = [y_p, y_s]
    for a, b in zip(st_p, st_s):
        out += [a, b]
    return tuple(out)
```

```python
import functools

import jax
import jax.numpy as jnp
from jax import lax
from jax.experimental import pallas as pl
from jax.experimental.pallas import tpu as pltpu

F32 = jnp.float32
BF16 = jnp.bfloat16
I32 = jnp.int32

D_MODEL = 1024
CHUNK = 64
CHUNK_SHIFT = 6
ROPE_THETA = 500000.0
LN_EPS = 1e-5
RMS_EPS = 1e-6
DEPTH_NOMINAL = 4
DN_ALPHA = (2 * DEPTH_NOMINAL) ** 0.25
N_HEADS = 8
HEAD_DIM = 64
Q_LORA = 384
KV_LORA = 256
ROPE_A = 32
ROT_C = 16
IDX_H = 8
IDX_DIM = 32
ROT_IDX = 8
TOPK_MAX = 256
N_BRANCH = 3
BRANCH_W = N_HEADS * HEAD_DIM
N_EXPERTS = 32
TOP_K = 4
D_FF = 1024
SWIGLU_LIMIT = 7.0
SWIGLU_ALPHA = 1.702

LANES = 128
VMEM_LIMIT_BYTES = 56 * 2**20

LOG2_E = 1.4426950408889634
NEG = -1e30
INT_MIN = -2**31
KEY_NEG_INF = -2139095041
BIG_IDX = 2**30


def _cparams(*sem):
    return pltpu.CompilerParams(dimension_semantics=sem, vmem_limit_bytes=VMEM_LIMIT_BYTES)


def _tile(n, pref):
    return pref if n % pref == 0 else n


def _full(shape):
    zeros = (0,) * len(shape)
    return pl.BlockSpec(shape, lambda *_: zeros)


def _dot(a, b):
    return jnp.dot(a, b, preferred_element_type=F32)


def _dot_nt(a, b):
    return lax.dot_general(a, b, (((1,), (1,)), ((), ())), preferred_element_type=F32)


def _layer_norm(x, g, b):
    xc = x - jnp.mean(x, axis=-1, keepdims=True)
    var = jnp.mean(xc * xc, axis=-1, keepdims=True)
    return xc * lax.rsqrt(var + LN_EPS) * g + b


def _rms_norm(x, g):
    return x * lax.rsqrt(jnp.mean(x * x, axis=-1, keepdims=True) + RMS_EPS) * g


def _rope_slab(x, c, sa, sb, half):
    return x * c + pltpu.roll(x, half, 1) * sa + pltpu.roll(x, LANES - half, 1) * sb


def _rope_tables(pos, rot, period):
    half = rot // 2
    inv_freq = ROPE_THETA ** (-jnp.arange(half, dtype=F32) / half)
    ang = pos.astype(F32)[:, None] * inv_freq
    j = jnp.arange(LANES) % period
    cos = jnp.cos(ang)[:, j % half]
    sin = jnp.sin(ang)[:, j % half]
    c = jnp.where(j < rot, cos, 1.0)
    sa = jnp.where((j >= half) & (j < rot), sin, 0.0)
    sb = jnp.where(j < half, -sin, 0.0)
    return c, sa, sb


def _ln_kernel(x_ref, g_ref, b_ref, o_ref, ob_ref):
    y = _layer_norm(x_ref[...], g_ref[...], b_ref[...])
    o_ref[...] = y
    ob_ref[...] = y.astype(BF16)


def _ln_in(x, g, b):
    n, d = x.shape
    tm = _tile(n, 512)
    row = pl.BlockSpec((tm, d), lambda i: (i, 0))
    return pl.pallas_call(
        _ln_kernel, grid=(n // tm,),
        in_specs=[row, _full((1, d)), _full((1, d))],
        out_specs=[row, row],
        out_shape=[jax.ShapeDtypeStruct((n, d), F32), jax.ShapeDtypeStruct((n, d), BF16)],
        compiler_params=_cparams("parallel"),
    )(x, g.reshape(1, d), b.reshape(1, d))


def _mla_proj_kernel(hb_ref, w_ref, wuq_ref, gqa_ref, gkva_ref, c_ref, sa_ref, sb_ref,
                     qn_ref, qr_ref, ckv_ref, kr_ref):
    z = _dot(hb_ref[0], w_ref[...])
    qa = _rms_norm(z[:, :Q_LORA], gqa_ref[...])
    q = _dot(qa.astype(BF16), wuq_ref[...])
    qn_ref[0] = q[:, :BRANCH_W].astype(BF16)
    c, sa, sb = c_ref[...], sa_ref[...], sb_ref[...]
    for s in range(2):
        lo = BRANCH_W + s * LANES
        qr_ref[0, :, s * LANES:(s + 1) * LANES] = _rope_slab(q[:, lo:lo + LANES], c, sa, sb, ROPE_A // 2).astype(BF16)
    ckv_ref[0] = _rms_norm(z[:, Q_LORA:Q_LORA + KV_LORA], gkva_ref[...])
    kr = _rope_slab(z[:, Q_LORA + KV_LORA:], c, sa, sb, ROPE_A // 2)
    kr_ref[0] = kr[:, :ROPE_A]


def _fox_proj_kernel(hb_ref, w_ref, bf_ref, q_ref, k_ref, v_ref, k16_ref, v16_ref, lf_ref):
    z = _dot(hb_ref[0], w_ref[...])
    w = BRANCH_W
    q_ref[0] = (z[:, :w] * HEAD_DIM ** -0.5).astype(BF16)
    k = z[:, w:2 * w]
    v = z[:, 2 * w:3 * w]
    k_ref[0] = k
    v_ref[0] = v
    k16_ref[0] = k.astype(BF16)
    v16_ref[0] = v.astype(BF16)
    x = z[:, 3 * w:3 * w + N_HEADS] + bf_ref[...]
    lf_ref[0] = jnp.minimum(x, 0.0) - jnp.log1p(jnp.exp(-jnp.abs(x)))


def _dsa_proj_kernel(hb_ref, w_ref, c64_ref, sa64_ref, sb64_ref, c32_ref, sa32_ref, sb32_ref,
                     q_ref, k_ref, v_ref, k16_ref, v16_ref, iq_ref, ik_ref, iw_ref):
    z = _dot(hb_ref[0], w_ref[...])
    w = BRANCH_W
    c64, sa64, sb64 = c64_ref[...], sa64_ref[...], sb64_ref[...]
    c32, sa32, sb32 = c32_ref[...], sa32_ref[...], sb32_ref[...]
    for s in range(w // LANES):
        sl = slice(s * LANES, (s + 1) * LANES)
        qs = _rope_slab(z[:, s * LANES:(s + 1) * LANES], c64, sa64, sb64, ROT_C // 2)
        q_ref[0, :, sl] = (qs * HEAD_DIM ** -0.5).astype(BF16)
        ks = _rope_slab(z[:, w + s * LANES:w + (s + 1) * LANES], c64, sa64, sb64, ROT_C // 2)
        k_ref[0, :, sl] = ks
        k16_ref[0, :, sl] = ks.astype(BF16)
    v = z[:, 2 * w:3 * w]
    v_ref[0] = v
    v16_ref[0] = v.astype(BF16)
    o = 3 * w
    for s in range(2):
        iq_ref[0, :, s * LANES:(s + 1) * LANES] = _rope_slab(
            z[:, o + s * LANES:o + (s + 1) * LANES], c32, sa32, sb32, ROT_IDX // 2).astype(BF16)
    o += IDX_H * IDX_DIM
    ik = _rope_slab(z[:, o:o + LANES], c32, sa32, sb32, ROT_IDX // 2)
    ik_ref[0] = ik[:, :IDX_DIM]
    o += LANES
    iw_ref[0] = z[:, o:o + IDX_H] * (IDX_DIM ** -0.5 * IDX_H ** -0.5)


def _proj_call(kernel, hb, weights, tables, out_widths_dtypes, tm):
    B, T, D = hb.shape
    grid = (B, T // tm)
    in_specs = [pl.BlockSpec((1, tm, D), lambda b, t: (b, t, 0))]
    in_specs += [_full(w.shape) for w in weights]
    in_specs += [pl.BlockSpec((tm, LANES), lambda b, t: (t, 0)) for _ in tables]
    out_specs = [pl.BlockSpec((1, tm, w), lambda b, t: (b, t, 0)) for w, _ in out_widths_dtypes]
    out_shape = [jax.ShapeDtypeStruct((B, T, w), dt) for w, dt in out_widths_dtypes]
    return pl.pallas_call(kernel, grid=grid, in_specs=in_specs, out_specs=out_specs, out_shape=out_shape,
                          compiler_params=_cparams("parallel", "parallel"))(hb, *weights, *tables)


def _mla_kv_kernel(ckv_ref, kr_ref, w_ref, tile_ref, kpp_ref, v_ref):
    kv = _dot(ckv_ref[0].astype(BF16), w_ref[...])
    kr4 = _dot(kr_ref[0].astype(BF16), tile_ref[...]).astype(BF16)
    for p in range(N_HEADS // 2):
        kpp_ref[0, p, :, :LANES] = kv[:, p * LANES:(p + 1) * LANES].astype(BF16)
        kpp_ref[0, p, :, LANES:] = kr4
    v_ref[0] = kv[:, BRANCH_W:].astype(BF16)


def _mla_kv(ckv, kr, w_ukv_p, tile_mat):
    B, Tk, _ = ckv.shape
    tm = _tile(Tk, 512)
    return pl.pallas_call(
        _mla_kv_kernel, grid=(B, Tk // tm),
        in_specs=[pl.BlockSpec((1, tm, KV_LORA), lambda b, t: (b, t, 0)),
                  pl.BlockSpec((1, tm, ROPE_A), lambda b, t: (b, t, 0)),
                  _full(w_ukv_p.shape), _full(tile_mat.shape)],
        out_specs=[pl.BlockSpec((1, N_HEADS // 2, tm, 2 * LANES), lambda b, t: (b, 0, t, 0)),
                   pl.BlockSpec((1, tm, BRANCH_W), lambda b, t: (b, t, 0))],
        out_shape=[jax.ShapeDtypeStruct((B, N_HEADS // 2, Tk, 2 * LANES), BF16),
                   jax.ShapeDtypeStruct((B, Tk, BRANCH_W), BF16)],
        compiler_params=_cparams("parallel", "parallel"),
    )(ckv, kr, w_ukv_p, tile_mat)


def _cumsum_kernel(x_ref, u_ref, o_ref, carry_ref):
    @pl.when(pl.program_id(1) == 0)
    def _():
        carry_ref[...] = jnp.zeros_like(carry_ref)

    x = x_ref[0]
    u = u_ref[...]
    hi = x.astype(BF16)
    r1 = x - hi.astype(F32)
    mid = r1.astype(BF16)
    lo = (r1 - mid.astype(F32)).astype(BF16)
    out = _dot(hi, u) + _dot(mid, u) + _dot(lo, u) + carry_ref[...]
    o_ref[0] = out
    tc = x.shape[1]
    carry_ref[...] = out[:, tc - 1:tc]


def _cumsum_time(x):
    B, H, Tk = x.shape
    tc = _tile(Tk, 512)
    u = (jnp.arange(tc)[:, None] <= jnp.arange(tc)[None, :]).astype(BF16)
    return pl.pallas_call(
        _cumsum_kernel, grid=(B, Tk // tc),
        in_specs=[pl.BlockSpec((1, H, tc), lambda b, t: (b, 0, t)), _full((tc, tc))],
        out_specs=pl.BlockSpec((1, H, tc), lambda b, t: (b, 0, t)),
        out_shape=jax.ShapeDtypeStruct((B, H, Tk), F32),
        scratch_shapes=[pltpu.VMEM((H, 1), F32)],
        compiler_params=_cparams("parallel", "arbitrary"),
    )(x, u)


def _own_lanes(lane, h, width):
    per = LANES // width
    return jnp.right_shift(lane, width.bit_length() - 1) == (h % per)


def _keep_head(x, lane, h, width):
    return jnp.where(_own_lanes(lane, h, width), x.astype(F32), 0.0).astype(BF16)


def _softmax_step(h, s, v_slab, m_ref, l_ref, acc_ref):
    m_prev = m_ref[h]
    m_new = jnp.maximum(m_prev, jnp.max(s, axis=1, keepdims=True))
    alpha = jnp.exp(m_prev - m_new)
    p = jnp.exp(s - m_new)
    l_ref[h] = alpha * l_ref[h] + jnp.sum(p, axis=1, keepdims=True)
    acc_ref[h] = alpha * acc_ref[h] + _dot(p.astype(BF16), v_slab)
    m_ref[h] = m_new


def _init_softmax(m_ref, l_ref, acc_ref):
    m_ref[...] = jnp.full(m_ref.shape, NEG, F32)
    l_ref[...] = jnp.zeros(l_ref.shape, F32)
    acc_ref[...] = jnp.zeros(acc_ref.shape, F32)


def _write_heads(o_ref, l_ref, acc_ref, lane):
    for p in range(N_HEADS // 2):
        a = acc_ref[2 * p] * (1.0 / l_ref[2 * p])
        b = acc_ref[2 * p + 1] * (1.0 / l_ref[2 * p + 1])
        o_ref[0, :, p * LANES:(p + 1) * LANES] = jnp.where(lane < HEAD_DIM, a, b).astype(BF16)


def _softmax_scratch(tq):
    return [pltpu.VMEM((N_HEADS, tq, 1), F32), pltpu.VMEM((N_HEADS, tq, 1), F32),
            pltpu.VMEM((N_HEADS, tq, LANES), F32)]


def _mla_attn_kernel(qn_ref, qr_ref, kpp_ref, v_ref, o_ref, qs_ref, m_ref, l_ref, acc_ref,
                     *, tq, tk, past, n_keys, padded):
    qi = pl.program_id(1)
    ki = pl.program_id(2)
    q0 = past + qi * tq
    last_k = (jnp.minimum(n_keys, ((q0 + tq - 1) // CHUNK + 1) * CHUNK) - 1) // tk
    lane = lax.broadcasted_iota(I32, (tq, LANES), 1)

    @pl.when(ki == 0)
    def _init():
        _init_softmax(m_ref, l_ref, acc_ref)
        for h in range(N_HEADS):
            qn = qn_ref[0, :, (h // 2) * LANES:(h // 2 + 1) * LANES]
            qr = qr_ref[0, :, (h // 4) * LANES:(h // 4 + 1) * LANES]
            qs_ref[h, :, :LANES] = _keep_head(qn, lane, h, HEAD_DIM)
            qs_ref[h, :, LANES:] = _keep_head(qr, lane, h, ROPE_A)

    @pl.when(ki <= last_k)
    def _step():
        qpos = q0 + lax.broadcasted_iota(I32, (tq, tk), 0)
        kpos = ki * tk + lax.broadcasted_iota(I32, (tq, tk), 1)
        kchunk = jnp.right_shift(kpos, CHUNK_SHIFT)
        if padded:
            kchunk = jnp.where(kpos < n_keys, kchunk, BIG_IDX)
        ok = kchunk <= jnp.right_shift(qpos, CHUNK_SHIFT)
        scale = (HEAD_DIM + ROPE_A) ** -0.5
        for h in range(N_HEADS):
            p = h // 2
            s = _dot_nt(qs_ref[h], kpp_ref[0, p]) * scale
            s = jnp.where(ok, s, NEG)
            _softmax_step(h, s, v_ref[0, :, p * LANES:(p + 1) * LANES], m_ref, l_ref, acc_ref)

    @pl.when(ki == pl.num_programs(2) - 1)
    def _fin():
        _write_heads(o_ref, l_ref, acc_ref, lane)


def _mla_attention(qn, qr, kpp, v, past, n_keys):
    B, T, _ = qn.shape
    Tkp = v.shape[1]
    tq = _tile(T, 512)
    tk = _tile(Tkp, 512)

    def kblk(qi, ki):
        last = (jnp.minimum(n_keys, ((past + qi * tq + tq - 1) // CHUNK + 1) * CHUNK) - 1) // tk
        return jnp.minimum(ki, last)

    kern = functools.partial(_mla_attn_kernel, tq=tq, tk=tk, past=past, n_keys=n_keys, padded=Tkp != n_keys)
    return pl.pallas_call(
        kern, grid=(B, T // tq, Tkp // tk),
        in_specs=[pl.BlockSpec((1, tq, BRANCH_W), lambda b, qi, ki: (b, qi, 0)),
                  pl.BlockSpec((1, tq, 2 * LANES), lambda b, qi, ki: (b, qi, 0)),
                  pl.BlockSpec((1, N_HEADS // 2, tk, 2 * LANES), lambda b, qi, ki: (b, 0, kblk(qi, ki), 0)),
                  pl.BlockSpec((1, tk, BRANCH_W), lambda b, qi, ki: (b, kblk(qi, ki), 0))],
        out_specs=pl.BlockSpec((1, tq, BRANCH_W), lambda b, qi, ki: (b, qi, 0)),
        out_shape=jax.ShapeDtypeStruct((B, T, BRANCH_W), BF16),
        scratch_shapes=[pltpu.VMEM((N_HEADS, tq, 2 * LANES), BF16)] + _softmax_scratch(tq),
        compiler_params=_cparams("parallel", "parallel", "arbitrary"),
    )(qn, qr, kpp, v)


def _fox_attn_kernel(q_ref, k_ref, v_ref, cq_ref, ck_ref, o_ref, qs_ref, m_ref, l_ref, acc_ref,
                     *, tq, tk, past):
    qi = pl.program_id(1)
    ki = pl.program_id(2)
    q0 = past + qi * tq
    last_k = (q0 + tq - 1) // tk
    lane = lax.broadcasted_iota(I32, (tq, LANES), 1)

    @pl.when(ki == 0)
    def _init():
        _init_softmax(m_ref, l_ref, acc_ref)
        for h in range(N_HEADS):
            q = q_ref[0, :, (h // 2) * LANES:(h // 2 + 1) * LANES]
            qs_ref[h] = _keep_head(q, lane, h, HEAD_DIM)

    @pl.when(ki <= last_k)
    def _step():
        qpos = q0 + lax.broadcasted_iota(I32, (tq, tk), 0)
        kpos = ki * tk + lax.broadcasted_iota(I32, (tq, tk), 1)
        ok = kpos <= qpos
        for h in range(N_HEADS):
            p = h // 2
            s = _dot_nt(qs_ref[h], k_ref[0, :, p * LANES:(p + 1) * LANES])
            s = s + cq_ref[0, :, h:h + 1] - ck_ref[0, h:h + 1, :]
            s = jnp.where(ok, s, NEG)
            _softmax_step(h, s, v_ref[0, :, p * LANES:(p + 1) * LANES], m_ref, l_ref, acc_ref)

    @pl.when(ki == pl.num_programs(2) - 1)
    def _fin():
        _write_heads(o_ref, l_ref, acc_ref, lane)


def _fox_attention(q, k, v, cq, ck, past):
    B, T, _ = q.shape
    Tkp = k.shape[1]
    tq = _tile(T, 512)
    tk = _tile(Tkp, 512)

    def kblk(qi, ki):
        return jnp.minimum(ki, (past + qi * tq + tq - 1) // tk)

    kern = functools.partial(_fox_attn_kernel, tq=tq, tk=tk, past=past)
    return pl.pallas_call(
        kern, grid=(B, T // tq, Tkp // tk),
        in_specs=[pl.BlockSpec((1, tq, BRANCH_W), lambda b, qi, ki: (b, qi, 0)),
                  pl.BlockSpec((1, tk, BRANCH_W), lambda b, qi, ki: (b, kblk(qi, ki), 0)),
                  pl.BlockSpec((1, tk, BRANCH_W), lambda b, qi, ki: (b, kblk(qi, ki), 0)),
                  pl.BlockSpec((1, tq, N_HEADS), lambda b, qi, ki: (b, qi, 0)),
                  pl.BlockSpec((1, N_HEADS, tk), lambda b, qi, ki: (b, 0, kblk(qi, ki)))],
        out_specs=pl.BlockSpec((1, tq, BRANCH_W), lambda b, qi, ki: (b, qi, 0)),
        out_shape=jax.ShapeDtypeStruct((B, T, BRANCH_W), BF16),
        scratch_shapes=[pltpu.VMEM((N_HEADS, tq, LANES), BF16)] + _softmax_scratch(tq),
        compiler_params=_cparams("parallel", "parallel", "arbitrary"),
    )(q, k, v, cq, ck)


def _dsa_attn_kernel(q_ref, k_ref, v_ref, iq_ref, ikt_ref, iw_ref, o_ref,
                     key_ref, qs_ref, iqs_ref, m_ref, l_ref, acc_ref, cidx_ref,
                     *, tq, tk, past, n_keys, padded, n_keep, idx_bits):
    qi = pl.program_id(1)
    q0 = past + qi * tq
    adm_end = jnp.minimum(n_keys, ((q0 + tq - 1) // CHUNK + 1) * CHUNK)
    nkb = (adm_end + tk - 1) // tk
    lane = lax.broadcasted_iota(I32, (tq, LANES), 1)
    qpos = q0 + lax.broadcasted_iota(I32, (tq, tk), 0)
    col = lax.broadcasted_iota(I32, (tq, tk), 1)

    _init_softmax(m_ref, l_ref, acc_ref)
    for h in range(N_HEADS):
        q = q_ref[0, :, (h // 2) * LANES:(h // 2 + 1) * LANES]
        qs_ref[h] = _keep_head(q, lane, h, HEAD_DIM)
        iq = iq_ref[0, :, (h // 4) * LANES:(h // 4 + 1) * LANES]
        iqs_ref[h] = _keep_head(iq, lane, h, IDX_DIM)

    def score_body(kb, carry):
        ikt = ikt_ref[0, kb]
        sc = jnp.zeros((tq, tk), F32)
        for h in range(IDX_H):
            sc = sc + iw_ref[0, :, h:h + 1] * jnp.maximum(_dot(iqs_ref[h], ikt), 0.0)
        kpos = kb * tk + col
        bits = pltpu.bitcast(sc + 0.0, I32)
        okey = jnp.where(bits < 0, bits ^ 0x7FFFFFFF, bits)
        kchunk = jnp.right_shift(kpos, CHUNK_SHIFT)
        if padded:
            kchunk = jnp.where(kpos < n_keys, kchunk, BIG_IDX)
        adm = kchunk <= jnp.right_shift(qpos, CHUNK_SHIFT)
        key_ref[kb] = jnp.where(adm, okey, KEY_NEG_INF)
        return carry

    lax.fori_loop(0, nkb, score_body, 0)

    def count(pred):
        def body(kb, c):
            return c + jnp.sum(pred(key_ref[kb], kb), axis=1, keepdims=True)
        return lax.fori_loop(0, nkb, body, jnp.zeros((tq, 1), F32))

    def ones_where(cond):
        return jnp.where(cond, 1.0, 0.0)

    keep = float(n_keep)
    c_nonneg = count(lambda kt, kb: ones_where(kt >= 0))
    base = jnp.where(c_nonneg >= keep, jnp.zeros((tq, 1), I32), jnp.full((tq, 1), INT_MIN, I32))

    def bit_body(j, base):
        cand = base | jnp.left_shift(jnp.int32(1), 30 - j)
        c = count(lambda kt, kb: ones_where(kt >= cand))
        return jnp.where(c >= keep, cand, base)

    thr = lax.fori_loop(0, 31, bit_body, base)

    c_gt = count(lambda kt, kb: ones_where(kt > thr))
    c_ge = count(lambda kt, kb: ones_where(kt >= thr))
    need = keep - c_gt
    few = thr == KEY_NEG_INF
    tied = jnp.where(few, 0.0, ones_where(c_ge > keep))
    cidx_ref[...] = jnp.where(few, -1, BIG_IDX)

    @pl.when(jnp.max(tied) > 0.5)
    def _ties():
        def idx_body(j, b):
            cand = b | jnp.left_shift(jnp.int32(1), idx_bits - 1 - j)
            f = count(lambda kt, kb: jnp.where(kt == thr, ones_where(kb * tk + col < cand), 0.0))
            return jnp.where(f < need, cand, b)
        b = lax.fori_loop(0, idx_bits, idx_body, jnp.zeros((tq, 1), I32))
        cidx_ref[...] = jnp.where(few, -1, jnp.where(tied > 0.5, b, BIG_IDX))

    cidx = cidx_ref[...]

    def attn_body(kb, carry):
        k0 = pl.multiple_of(kb * tk, tk)
        kt = key_ref[kb]
        kpos = kb * tk + col
        bias = jnp.where(kt > thr, 0.0, jnp.where(kt == thr, jnp.where(kpos <= cidx, 0.0, NEG), NEG))
        for h in range(N_HEADS):
            p = h // 2
            s = _dot_nt(qs_ref[h], k_ref[0, pl.ds(k0, tk), p * LANES:(p + 1) * LANES]) + bias
            _softmax_step(h, s, v_ref[0, pl.ds(k0, tk), p * LANES:(p + 1) * LANES], m_ref, l_ref, acc_ref)
        return carry

    lax.fori_loop(0, nkb, attn_body, 0)
    _write_heads(o_ref, l_ref, acc_ref, lane)


def _dsa_attention(q, k, v, iq, ikt, iw, past, n_keys):
    B, T, _ = q.shape
    Tkp = k.shape[1]
    tq = _tile(T, 512)
    tk = _tile(Tkp, 512)
    nkb = Tkp // tk
    n_keep = min(TOPK_MAX, n_keys // 4)
    idx_bits = max(1, (Tkp - 1).bit_length())
    kern = functools.partial(_dsa_attn_kernel, tq=tq, tk=tk, past=past, n_keys=n_keys, padded=Tkp != n_keys,
                             n_keep=n_keep, idx_bits=idx_bits)
    return pl.pallas_call(
        kern, grid=(B, T // tq),
        in_specs=[pl.BlockSpec((1, tq, BRANCH_W), lambda b, qi: (b, qi, 0)),
                  pl.BlockSpec((1, Tkp, BRANCH_W), lambda b, qi: (b, 0, 0)),
                  pl.BlockSpec((1, Tkp, BRANCH_W), lambda b, qi: (b, 0, 0)),
                  pl.BlockSpec((1, tq, IDX_H * IDX_DIM), lambda b, qi: (b, qi, 0)),
                  pl.BlockSpec((1, nkb, LANES, tk), lambda b, qi: (b, 0, 0, 0)),
                  pl.BlockSpec((1, tq, IDX_H), lambda b, qi: (b, qi, 0))],
        out_specs=pl.BlockSpec((1, tq, BRANCH_W), lambda b, qi: (b, qi, 0)),
        out_shape=jax.ShapeDtypeStruct((B, T, BRANCH_W), BF16),
        scratch_shapes=[pltpu.VMEM((nkb, tq, tk), I32),
                        pltpu.VMEM((N_HEADS, tq, LANES), BF16),
                        pltpu.VMEM((IDX_H, tq, LANES), BF16)] + _softmax_scratch(tq)
                       + [pltpu.VMEM((tq, 1), I32)],
        compiler_params=_cparams("parallel", "arbitrary"),
    )(q, k, v, iq, ikt, iw)


def _heads_t(x_ref, qt_ref, lane, width, row0=0):
    per = LANES // width
    for h in range(N_HEADS):
        s = h // per
        x = x_ref[0, :, s * LANES:(s + 1) * LANES].astype(F32)
        x = jnp.where(_own_lanes(lane, h, width), x, 0.0)
        qt_ref[h, row0:row0 + LANES, :] = x.T.astype(BF16)


def _softmax_step_t(h, s, vt_slab, m_ref, l_ref, acc_ref, exp=jnp.exp):
    m_prev = m_ref[h]
    m_new = jnp.maximum(m_prev, jnp.max(s, axis=0, keepdims=True))
    alpha = exp(m_prev - m_new)
    p = exp(s - m_new)
    l_ref[h] = alpha * l_ref[h] + jnp.sum(p, axis=0, keepdims=True)
    acc_ref[h] = alpha * acc_ref[h] + _dot(vt_slab, p.astype(BF16))
    m_ref[h] = m_new


def _write_heads_t(o_ref, l_ref, acc_ref, tq):
    row = lax.broadcasted_iota(I32, (LANES, tq), 0)
    for p in range(N_HEADS // 2):
        a = acc_ref[2 * p] * (1.0 / l_ref[2 * p])
        b = acc_ref[2 * p + 1] * (1.0 / l_ref[2 * p + 1])
        o_ref[0, :, p * LANES:(p + 1) * LANES] = jnp.where(row < HEAD_DIM, a, b).T.astype(BF16)


def _softmax_scratch_t(tq):
    return [pltpu.VMEM((N_HEADS, 1, tq), F32), pltpu.VMEM((N_HEADS, 1, tq), F32),
            pltpu.VMEM((N_HEADS, LANES, tq), F32)]


def _mla_attn_t_kernel(qn_ref, qr_ref, kpp_ref, vt_ref, o_ref, qt_ref, m_ref, l_ref, acc_ref,
                       *, tq, tk, past, n_keys):
    qi = pl.program_id(1)
    ki = pl.program_id(2)
    q0 = past + qi * tq
    last_k = (jnp.minimum(n_keys, ((q0 + tq - 1) // CHUNK + 1) * CHUNK) - 1) // tk

    @pl.when(ki == 0)
    def _init():
        _init_softmax(m_ref, l_ref, acc_ref)
        lane = lax.broadcasted_iota(I32, (tq, LANES), 1)
        _heads_t(qn_ref, qt_ref, lane, HEAD_DIM)
        _heads_t(qr_ref, qt_ref, lane, ROPE_A, row0=LANES)

    interior = (ki * tk + tk - 1) // CHUNK <= q0 // CHUNK

    def block(masked):
        scale = (HEAD_DIM + ROPE_A) ** -0.5 * LOG2_E
        if masked:
            kpos = ki * tk + lax.broadcasted_iota(I32, (tk, tq), 0)
            qpos = q0 + lax.broadcasted_iota(I32, (tk, tq), 1)
            ok = jnp.right_shift(kpos, CHUNK_SHIFT) <= jnp.right_shift(qpos, CHUNK_SHIFT)
        for h in range(N_HEADS):
            p = h // 2
            s = _dot(kpp_ref[0, p], qt_ref[h]) * scale
            if masked:
                s = jnp.where(ok, s, NEG)
            _softmax_step_t(h, s, vt_ref[0, 0, p * LANES:(p + 1) * LANES, :], m_ref, l_ref, acc_ref, exp=jnp.exp2)

    @pl.when(jnp.logical_and(ki <= last_k, interior))
    def _interior():
        block(False)

    @pl.when(jnp.logical_and(ki <= last_k, jnp.logical_not(interior)))
    def _diagonal():
        block(True)

    @pl.when(ki == pl.num_programs(2) - 1)
    def _fin():
        _write_heads_t(o_ref, l_ref, acc_ref, tq)


def _fox_attn_t_kernel(q_ref, k_ref, vt_ref, cq_ref, ck_ref, o_ref, qt_ref, m_ref, l_ref, acc_ref,
                       *, tq, tk, past):
    qi = pl.program_id(1)
    ki = pl.program_id(2)
    q0 = past + qi * tq
    last_k = (q0 + tq - 1) // tk

    @pl.when(ki == 0)
    def _init():
        _init_softmax(m_ref, l_ref, acc_ref)
        _heads_t(q_ref, qt_ref, lax.broadcasted_iota(I32, (tq, LANES), 1), HEAD_DIM)

    interior = ki * tk + tk - 1 <= q0

    def block(masked):
        if masked:
            kpos = ki * tk + lax.broadcasted_iota(I32, (tk, tq), 0)
            qpos = q0 + lax.broadcasted_iota(I32, (tk, tq), 1)
            ok = kpos <= qpos
        for h in range(N_HEADS):
            p = h // 2
            s = _dot(k_ref[0, :, p * LANES:(p + 1) * LANES], qt_ref[h])
            s = s + cq_ref[0, h:h + 1, :] - ck_ref[0, :, h:h + 1]
            if masked:
                s = jnp.where(ok, s, NEG)
            _softmax_step_t(h, s, vt_ref[0, 0, p * LANES:(p + 1) * LANES, :], m_ref, l_ref, acc_ref)

    @pl.when(jnp.logical_and(ki <= last_k, interior))
    def _interior():
        block(False)

    @pl.when(jnp.logical_and(ki <= last_k, jnp.logical_not(interior)))
    def _diagonal():
        block(True)

    @pl.when(ki == pl.num_programs(2) - 1)
    def _fin():
        _write_heads_t(o_ref, l_ref, acc_ref, tq)


def _blocked_t(v, tk):
    B, Tk, C = v.shape
    return jnp.swapaxes(v.reshape(B, Tk // tk, tk, C), 2, 3)


def _mla_attention_t(qn, qr, kpp, v, past, n_keys):
    B, T, _ = qn.shape
    Tk = v.shape[1]
    tq = _tile(T, 512)
    tk = _tile(Tk, 512)

    def kblk(qi, ki):
        last = (jnp.minimum(n_keys, ((past + qi * tq + tq - 1) // CHUNK + 1) * CHUNK) - 1) // tk
        return jnp.minimum(ki, last)

    kern = functools.partial(_mla_attn_t_kernel, tq=tq, tk=tk, past=past, n_keys=n_keys)
    return pl.pallas_call(
        kern, grid=(B, T // tq, Tk // tk),
        in_specs=[pl.BlockSpec((1, tq, BRANCH_W), lambda b, qi, ki: (b, qi, 0)),
                  pl.BlockSpec((1, tq, 2 * LANES), lambda b, qi, ki: (b, qi, 0)),
                  pl.BlockSpec((1, N_HEADS // 2, tk, 2 * LANES), lambda b, qi, ki: (b, 0, kblk(qi, ki), 0)),
                  pl.BlockSpec((1, 1, BRANCH_W, tk), lambda b, qi, ki: (b, kblk(qi, ki), 0, 0))],
        out_specs=pl.BlockSpec((1, tq, BRANCH_W), lambda b, qi, ki: (b, qi, 0)),
        out_shape=jax.ShapeDtypeStruct((B, T, BRANCH_W), BF16),
        scratch_shapes=[pltpu.VMEM((N_HEADS, 2 * LANES, tq), BF16)] + _softmax_scratch_t(tq),
        compiler_params=_cparams("parallel", "parallel", "arbitrary"),
    )(qn, qr, kpp, _blocked_t(v, tk))


def _fox_attention_t(q, k, v, cq_t, ck, past):
    B, T, _ = q.shape
    Tk = k.shape[1]
    tq = _tile(T, 512)
    tk = _tile(Tk, 512)

    def kblk(qi, ki):
        return jnp.minimum(ki, (past + qi * tq + tq - 1) // tk)

    kern = functools.partial(_fox_attn_t_kernel, tq=tq, tk=tk, past=past)
    return pl.pallas_call(
        kern, grid=(B, T // tq, Tk // tk),
        in_specs=[pl.BlockSpec((1, tq, BRANCH_W), lambda b, qi, ki: (b, qi, 0)),
                  pl.BlockSpec((1, tk, BRANCH_W), lambda b, qi, ki: (b, kblk(qi, ki), 0)),
                  pl.BlockSpec((1, 1, BRANCH_W, tk), lambda b, qi, ki: (b, kblk(qi, ki), 0, 0)),
                  pl.BlockSpec((1, N_HEADS, tq), lambda b, qi, ki: (b, 0, qi)),
                  pl.BlockSpec((1, tk, N_HEADS), lambda b, qi, ki: (b, kblk(qi, ki), 0))],
        out_specs=pl.BlockSpec((1, tq, BRANCH_W), lambda b, qi, ki: (b, qi, 0)),
        out_shape=jax.ShapeDtypeStruct((B, T, BRANCH_W), BF16),
        scratch_shapes=[pltpu.VMEM((N_HEADS, LANES, tq), BF16)] + _softmax_scratch_t(tq),
        compiler_params=_cparams("parallel", "parallel", "arbitrary"),
    )(q, k, _blocked_t(v, tk), cq_t, ck)


def _dsa_attn_t_kernel(q_ref, k_ref, vt_ref, iq_ref, ik_ref, iw_ref, o_ref,
                       key_ref, qt_ref, iqt_ref, m_ref, l_ref, acc_ref, cidx_ref,
                       *, tq, tk, past, n_keys, n_keep, idx_bits):
    qi = pl.program_id(1)
    q0 = past + qi * tq
    adm_end = jnp.minimum(n_keys, ((q0 + tq - 1) // CHUNK + 1) * CHUNK)
    nkb = (adm_end + tk - 1) // tk
    lane = lax.broadcasted_iota(I32, (tq, LANES), 1)
    qchunk = jnp.right_shift(q0 + lax.broadcasted_iota(I32, (tk, tq), 1), CHUNK_SHIFT)
    krow = lax.broadcasted_iota(I32, (tk, tq), 0)

    _init_softmax(m_ref, l_ref, acc_ref)
    _heads_t(q_ref, qt_ref, lane, HEAD_DIM)
    _heads_t(iq_ref, iqt_ref, lane, IDX_DIM)

    def score_body(kb, carry):
        k0 = pl.multiple_of(kb * tk, tk)
        ik = ik_ref[0, pl.ds(k0, tk), :]
        sc = jnp.zeros((tk, tq), F32)
        for h in range(IDX_H):
            sc = sc + iw_ref[0, h:h + 1, :] * jnp.maximum(_dot(ik, iqt_ref[h]), 0.0)
        bits = pltpu.bitcast(sc + 0.0, I32)
        okey = jnp.where(bits < 0, bits ^ 0x7FFFFFFF, bits)
        adm = jnp.right_shift(k0 + krow, CHUNK_SHIFT) <= qchunk
        key_ref[kb] = jnp.where(adm, okey, KEY_NEG_INF)
        return carry

    lax.fori_loop(0, nkb, score_body, 0)

    def count(pred):
        def body(kb, c):
            return c + jnp.sum(pred(key_ref[kb], kb), axis=0, keepdims=True)
        return lax.fori_loop(0, nkb, body, jnp.zeros((1, tq), F32))

    def ones_where(cond):
        return jnp.where(cond, 1.0, 0.0)

    keep = float(n_keep)
    c_nonneg = count(lambda kt, kb: ones_where(kt >= 0))
    base = jnp.where(c_nonneg >= keep, jnp.zeros((1, tq), I32), jnp.full((1, tq), INT_MIN, I32))

    def bit_body(j, base):
        cand = base | jnp.left_shift(jnp.int32(1), 30 - j)
        c = count(lambda kt, kb: ones_where(kt >= cand))
        return jnp.where(c >= keep, cand, base)

    thr = lax.fori_loop(0, 31, bit_body, base)

    c_gt = count(lambda kt, kb: ones_where(kt > thr))
    c_ge = count(lambda kt, kb: ones_where(kt >= thr))
    need = keep - c_gt
    few = thr == KEY_NEG_INF
    tied = jnp.where(few, 0.0, ones_where(c_ge > keep))
    cidx_ref[...] = jnp.where(few, -1, BIG_IDX)

    @pl.when(jnp.max(tied) > 0.5)
    def _ties():
        def idx_body(j, b):
            cand = b | jnp.left_shift(jnp.int32(1), idx_bits - 1 - j)
            f = count(lambda kt, kb: jnp.where(kt == thr, ones_where(kb * tk + krow < cand), 0.0))
            return jnp.where(f < need, cand, b)
        b = lax.fori_loop(0, idx_bits, idx_body, jnp.zeros((1, tq), I32))
        cidx_ref[...] = jnp.where(few, -1, jnp.where(tied > 0.5, b, BIG_IDX))

    cidx = cidx_ref[...]

    def attn_body(kb, carry):
        k0 = pl.multiple_of(kb * tk, tk)
        kt = key_ref[kb]
        kpos = k0 + krow
        bias = jnp.where(kt > thr, 0.0, jnp.where(kt == thr, jnp.where(kpos <= cidx, 0.0, NEG), NEG))
        for h in range(N_HEADS):
            p = h // 2
            s = _dot(k_ref[0, pl.ds(k0, tk), p * LANES:(p + 1) * LANES], qt_ref[h]) + bias
            _softmax_step_t(h, s, vt_ref[0, kb, p * LANES:(p + 1) * LANES, :], m_ref, l_ref, acc_ref)
        return carry

    lax.fori_loop(0, nkb, attn_body, 0)
    _write_heads_t(o_ref, l_ref, acc_ref, tq)


def _dsa_attention_t(q, k, v, iq, ik4, iw_t, past, n_keys):
    B, T, _ = q.shape
    Tk = k.shape[1]
    tq = _tile(T, 512)
    tk = _tile(Tk, 512)
    nkb = Tk // tk
    n_keep = min(TOPK_MAX, n_keys // 4)
    idx_bits = max(1, (Tk - 1).bit_length())
    kern = functools.partial(_dsa_attn_t_kernel, tq=tq, tk=tk, past=past, n_keys=n_keys,
                             n_keep=n_keep, idx_bits=idx_bits)
    return pl.pallas_call(
        kern, grid=(B, T // tq),
        in_specs=[pl.BlockSpec((1, tq, BRANCH_W), lambda b, qi: (b, qi, 0)),
                  pl.BlockSpec((1, Tk, BRANCH_W), lambda b, qi: (b, 0, 0)),
                  pl.BlockSpec((1, nkb, BRANCH_W, tk), lambda b, qi: (b, 0, 0, 0)),
                  pl.BlockSpec((1, tq, IDX_H * IDX_DIM), lambda b, qi: (b, qi, 0)),
                  pl.BlockSpec((1, Tk, LANES), lambda b, qi: (b, 0, 0)),
                  pl.BlockSpec((1, IDX_H, tq), lambda b, qi: (b, 0, qi))],
        out_specs=pl.BlockSpec((1, tq, BRANCH_W), lambda b, qi: (b, qi, 0)),
        out_shape=jax.ShapeDtypeStruct((B, T, BRANCH_W), BF16),
        scratch_shapes=[pltpu.VMEM((nkb, tk, tq), I32),
                        pltpu.VMEM((N_HEADS, LANES, tq), BF16),
                        pltpu.VMEM((IDX_H, LANES, tq), BF16)] + _softmax_scratch_t(tq)
                       + [pltpu.VMEM((1, tq), I32)],
        compiler_params=_cparams("parallel", "arbitrary"),
    )(q, k, _blocked_t(v, tk), iq, ik4, iw_t)


def _merge_kernel(h_ref, hb_ref, oa_ref, ob_ref, oc_ref, wg_ref, bg_ref, wbr_ref, wout_ref, g_ref, b_ref,
                  wr_ref, br_ref, x_ref, xb_ref, idx_ref, gate_ref):
    hb = hb_ref[...]
    d = D_MODEL
    merged = None
    for n, o_ref in enumerate((oa_ref, ob_ref, oc_ref)):
        gt = _dot(hb, wg_ref[:, n * d:(n + 1) * d]) + bg_ref[:, n * d:(n + 1) * d]
        term = (1.0 / (1.0 + jnp.exp(-gt))) * _dot(o_ref[...], wbr_ref[n])
        merged = term if merged is None else merged + term
    out = _dot(merged.astype(BF16), wout_ref[...])
    x = _layer_norm(DN_ALPHA * h_ref[...] + out, g_ref[...], b_ref[...])
    xb = x.astype(BF16)
    x_ref[...] = x
    xb_ref[...] = xb

    logits = _dot(xb, wr_ref[...]) + br_ref[...]
    tm = logits.shape[0]
    lane = lax.broadcasted_iota(I32, (tm, N_EXPERTS), 1).astype(F32)
    k4 = lax.broadcasted_iota(I32, (tm, TOP_K), 1)
    vals = jnp.zeros((tm, TOP_K), F32)
    idxs = jnp.zeros((tm, TOP_K), F32)
    for k in range(TOP_K):
        mx = jnp.max(logits, axis=1, keepdims=True)
        ix = jnp.min(jnp.where(logits == mx, lane, float(N_EXPERTS)), axis=1, keepdims=True)
        vals = jnp.where(k4 == k, mx, vals)
        idxs = jnp.where(k4 == k, ix, idxs)
        logits = jnp.where(lane == ix, -jnp.inf, logits)
    e = jnp.exp(vals - jnp.max(vals, axis=1, keepdims=True))
    idx_ref[...] = idxs.astype(I32)
    gate_ref[...] = e * (1.0 / jnp.sum(e, axis=1, keepdims=True))


def _merge(h, hb, oa, ob, oc, wg, bg, wbr, wout, g, b, wr, br):
    n, d = h.shape
    tm = _tile(n, 256)
    row = lambda w: pl.BlockSpec((tm, w), lambda i: (i, 0))
    return pl.pallas_call(
        _merge_kernel, grid=(n // tm,),
        in_specs=[row(d), row(d), row(BRANCH_W), row(BRANCH_W), row(BRANCH_W),
                  _full(wg.shape), _full(bg.shape), _full(wbr.shape), _full(wout.shape),
                  _full(g.shape), _full(b.shape), _full(wr.shape), _full(br.shape)],
        out_specs=[row(d), row(d), row(TOP_K), row(TOP_K)],
        out_shape=[jax.ShapeDtypeStruct((n, d), F32), jax.ShapeDtypeStruct((n, d), BF16),
                   jax.ShapeDtypeStruct((n, TOP_K), I32), jax.ShapeDtypeStruct((n, TOP_K), F32)],
        compiler_params=_cparams("parallel"),
    )(h, hb, oa, ob, oc, wg, bg, wbr, wout, g, b, wr, br)


def _expert_kernel(blk_e_ref, idx_hbm, x_hbm, w1_ref, b1_ref, w2_ref, b2_ref, y_hbm,
                   idx_smem, xbuf, ybuf, isem, gsem, ssem, *, bs, n_blocks):
    del blk_e_ref
    i = pl.program_id(0)

    def idx_copy(row, s):
        return pltpu.make_async_copy(idx_hbm.at[row], idx_smem.at[s], isem.at[s])

    def start_gather(s):
        for r in range(bs):
            tok = idx_smem[s, r]
            pltpu.make_async_copy(x_hbm.at[pl.ds(tok, 1)], xbuf.at[s, pl.ds(r, 1)], gsem.at[s]).start()

    def wait_gather(s):
        pltpu.make_async_copy(x_hbm.at[pl.ds(0, bs)], xbuf.at[s], gsem.at[s]).wait()

    def start_scatter(s_buf, s_idx):
        for r in range(bs):
            dst = idx_smem[s_idx, bs + r]
            pltpu.make_async_copy(ybuf.at[s_buf, pl.ds(r, 1)], y_hbm.at[pl.ds(dst, 1)], ssem.at[s_buf]).start()

    def wait_scatter(s):
        pltpu.make_async_copy(ybuf.at[s], y_hbm.at[pl.ds(0, bs)], ssem.at[s]).wait()

    def step(slot, gather_next, scatter_prev):
        nslot = 1 - slot
        idx_copy(i + 2, slot).start()
        idx_copy(i + 1, nslot).wait()
        wait_gather(slot)

        @pl.when(i >= 2)
        def _():
            wait_scatter(slot)

        if gather_next:
            start_gather(nslot)
        if scatter_prev:
            start_scatter(nslot, nslot)
        x = xbuf[slot].astype(BF16)
        gu = _dot(x, w1_ref[0]) + b1_ref[0]
        g = jnp.minimum(gu[:, :D_FF], SWIGLU_LIMIT)
        u = jnp.clip(gu[:, D_FF:], -SWIGLU_LIMIT, SWIGLU_LIMIT)
        a = g * (1.0 / (1.0 + jnp.exp(-SWIGLU_ALPHA * g))) * (u + 1.0)
        ybuf[slot] = _dot(a.astype(BF16), w2_ref[0]) + b2_ref[0]

    last = n_blocks - 1
    middle = jnp.logical_and(i > 0, i < last)

    @pl.when(i == 0)
    def _first():
        idx_copy(0, 0).start()
        idx_copy(0, 0).wait()
        start_gather(0)
        idx_copy(1, 1).start()
        step(0, True, False)

    @pl.when(jnp.logical_and(middle, i % 2 == 0))
    def _middle_even():
        step(0, True, True)

    @pl.when(jnp.logical_and(middle, i % 2 == 1))
    def _middle_odd():
        step(1, True, True)

    @pl.when(i == last)
    def _last():
        slot = last % 2
        step(slot, False, True)
        idx_copy(i + 2, slot).wait()
        start_scatter(slot, slot)
        wait_scatter(1 - slot)
        wait_scatter(slot)


def _experts(x, blk_e, idx, w1, b1, w2, b2, n_rows_out, bs):
    n, d = x.shape
    n_blocks = idx.shape[0] - 2
    assert n_blocks >= 2
    kern = functools.partial(_expert_kernel, bs=bs, n_blocks=n_blocks)
    grid_spec = pltpu.PrefetchScalarGridSpec(
        num_scalar_prefetch=1, grid=(n_blocks,),
        in_specs=[pl.BlockSpec(memory_space=pl.ANY), pl.BlockSpec(memory_space=pl.ANY),
                  pl.BlockSpec((1, d, 2 * D_FF), lambda i, be: (be[i], 0, 0)),
                  pl.BlockSpec((1, 1, 2 * D_FF), lambda i, be: (be[i], 0, 0)),
                  pl.BlockSpec((1, D_FF, d), lambda i, be: (be[i], 0, 0)),
                  pl.BlockSpec((1, 1, d), lambda i, be: (be[i], 0, 0))],
        out_specs=pl.BlockSpec(memory_space=pl.ANY),
        scratch_shapes=[pltpu.SMEM((2, 2 * bs), I32),
                        pltpu.VMEM((2, bs, d), F32), pltpu.VMEM((2, bs, d), F32),
                        pltpu.SemaphoreType.DMA((2,)), pltpu.SemaphoreType.DMA((2,)),
                        pltpu.SemaphoreType.DMA((2,))])
    return pl.pallas_call(
        kern, grid_spec=grid_spec,
        out_shape=jax.ShapeDtypeStruct((n_rows_out, d), F32),
        compiler_params=_cparams("arbitrary"),
    )(blk_e, idx, x, w1, b1, w2, b2)


def _dispatch_tables(top_idx, bs):
    n = top_idx.shape[0]
    m = n * TOP_K
    flat_e = top_idx.reshape(-1)
    order = jnp.argsort(flat_e).astype(I32)
    counts = jnp.bincount(flat_e, length=N_EXPERTS).astype(I32)
    padded = (counts + bs - 1) // bs * bs
    start = jnp.cumsum(counts) - counts
    pend = jnp.cumsum(padded)
    pstart = pend - padded
    n_blocks = -(-m // bs) + N_EXPERTS
    cap = n_blocks * bs
    blk_e = jnp.minimum(jnp.searchsorted(pend, jnp.arange(n_blocks, dtype=I32) * bs, side='right'),
                        N_EXPERTS - 1).astype(I32)
    slot = jnp.arange(cap, dtype=I32)
    per_slot = lambda per_blk: jnp.broadcast_to(per_blk[:, None], (n_blocks, bs)).reshape(cap)
    e = per_slot(blk_e)
    is_pad = slot - per_slot(pstart[blk_e]) >= per_slot(counts[blk_e])
    ext = jnp.concatenate([jnp.zeros((cap,), I32), order, jnp.zeros((cap,), I32)])
    shift = pstart - start

    def place(ex, acc):
        return jnp.where(e == ex, lax.dynamic_slice(ext, (cap - shift[ex],), (cap,)), acc)

    flat = lax.fori_loop(0, N_EXPERTS, place, jnp.zeros((cap,), I32))
    src = jnp.where(is_pad, 0, flat // TOP_K)
    pads_before = slot - per_slot((start + counts)[blk_e])
    dst = jnp.where(is_pad, m + pads_before, (flat % TOP_K) * n + flat // TOP_K)
    zeros = jnp.zeros((2, bs), I32)
    idx = jnp.concatenate([jnp.concatenate([src.reshape(n_blocks, bs), zeros], axis=0),
                           jnp.concatenate([zeros, dst.reshape(n_blocks, bs)], axis=0)], axis=1)
    return blk_e, idx, cap


def _combine_kernel(x_ref, y0_ref, y1_ref, y2_ref, y3_ref, gate_ref, g_ref, b_ref, o_ref, ob_ref):
    y = gate_ref[:, 0:1] * y0_ref[...]
    for k, y_ref in enumerate((y1_ref, y2_ref, y3_ref), start=1):
        y = y + gate_ref[:, k:k + 1] * y_ref[...]
    out = _layer_norm(DN_ALPHA * x_ref[...] + y, g_ref[...], b_ref[...])
    o_ref[...] = out
    ob_ref[...] = out.astype(BF16)


def _combine(x, y, gate, g, b):
    n, d = x.shape
    tm = _tile(n, 256)
    nt = n // tm
    row = lambda w: pl.BlockSpec((tm, w), lambda i: (i, 0))
    y_specs = [pl.BlockSpec((tm, d), lambda i, k=k: (k * nt + i, 0)) for k in range(TOP_K)]
    return pl.pallas_call(
        _combine_kernel, grid=(nt,),
        in_specs=[row(d)] + y_specs + [row(TOP_K), _full((1, d)), _full((1, d))],
        out_specs=[row(d), row(d)],
        out_shape=[jax.ShapeDtypeStruct((n, d), F32), jax.ShapeDtypeStruct((n, d), BF16)],
        compiler_params=_cparams("parallel"),
    )(x, y, y, y, y, gate, g, b)


def _deinterleave_kernel(w_ref, sel_ref, o_ref):
    half = o_ref.shape[2] // 2
    for s in range(half // LANES):
        y = _dot(w_ref[0, :, 2 * s * LANES:2 * (s + 1) * LANES].astype(BF16), sel_ref[...])
        o_ref[0, :, s * LANES:(s + 1) * LANES] = y[:, :LANES].astype(BF16)
        o_ref[0, :, half + s * LANES:half + (s + 1) * LANES] = y[:, LANES:].astype(BF16)


def _deinterleave_up(w_up):
    L, E, d, f2 = w_up.shape
    j = jnp.arange(2 * LANES)
    sel = (j[:, None] == jnp.where(j < LANES, 2 * j, 2 * (j - LANES) + 1)[None, :]).astype(BF16)
    out = pl.pallas_call(
        _deinterleave_kernel, grid=(L * E,),
        in_specs=[pl.BlockSpec((1, d, f2), lambda i: (i, 0, 0)), _full((2 * LANES, 2 * LANES))],
        out_specs=pl.BlockSpec((1, d, f2), lambda i: (i, 0, 0)),
        out_shape=jax.ShapeDtypeStruct((L * E, d, f2), BF16),
        compiler_params=_cparams("parallel"),
    )(w_up.reshape(L * E, d, f2), sel)
    return out.reshape(L, E, d, f2)


def _prep_layer(l, w_in, b_f, b_gate, g_qa, g_kva, w_uq, w_ukv, w_br, w_out, ln1_g, ln1_b,
                w_router, b_router, w_up, b_up, w_down, b_down, ln2_g, ln2_b):
    w = w_in[l]
    o = [0]

    def take(width):
        s = w[:, o[0]:o[0] + width]
        o[0] += width
        return s

    def pad_to(a, width):
        return jnp.pad(a, ((0, 0), (0, width - a.shape[1])))

    qa, kva, kra = take(Q_LORA), take(KV_LORA), take(ROPE_A)
    qb, kb, vb, fb = take(BRANCH_W), take(BRANCH_W), take(BRANCH_W), take(N_HEADS)
    qc, kc, vc = take(BRANCH_W), take(BRANCH_W), take(BRANCH_W)
    iqc, ikc, iwc = take(IDX_H * IDX_DIM), take(IDX_DIM), take(IDX_H)
    gt = take(N_BRANCH * D_MODEL)
    p = {}
    p['w_mla'] = jnp.concatenate([qa, kva, jnp.tile(kra, (1, 4))], axis=1).astype(BF16)
    p['w_fox'] = jnp.concatenate([qb, kb, vb, pad_to(fb, LANES)], axis=1).astype(BF16)
    p['w_dsa'] = jnp.concatenate([qc, kc, vc, iqc, jnp.tile(ikc, (1, 4)), pad_to(iwc, LANES)], axis=1).astype(BF16)
    p['w_gate'] = gt.astype(BF16)
    uq = w_uq[l].reshape(Q_LORA, N_HEADS, HEAD_DIM + ROPE_A)
    p['w_uq'] = jnp.concatenate([uq[:, :, :HEAD_DIM].reshape(Q_LORA, -1),
                                 uq[:, :, HEAD_DIM:].reshape(Q_LORA, -1)], axis=1).astype(BF16)
    ukv = w_ukv[l].reshape(KV_LORA, N_HEADS, 2 * HEAD_DIM)
    p['w_ukv'] = jnp.concatenate([ukv[:, :, :HEAD_DIM].reshape(KV_LORA, -1),
                                  ukv[:, :, HEAD_DIM:].reshape(KV_LORA, -1)], axis=1).astype(BF16)
    p['b_f'] = b_f[l].reshape(1, N_HEADS)
    p['b_gate'] = b_gate[l].reshape(1, N_BRANCH * D_MODEL)
    p['g_qa'] = g_qa[l].reshape(1, Q_LORA)
    p['g_kva'] = g_kva[l].reshape(1, KV_LORA)
    p['w_br'] = w_br[l].astype(BF16)
    p['w_out'] = w_out[l].astype(BF16)
    p['ln1'] = (ln1_g[l].reshape(1, -1), ln1_b[l].reshape(1, -1))
    p['ln2'] = (ln2_g[l].reshape(1, -1), ln2_b[l].reshape(1, -1))
    p['w_router'] = w_router[l].astype(BF16)
    p['b_router'] = b_router[l].reshape(1, N_EXPERTS)
    p['w_up'] = w_up[l]
    p['b_up'] = jnp.concatenate([b_up[l][:, 0::2], b_up[l][:, 1::2]], axis=1)[:, None, :]
    p['w_down'] = w_down[l].astype(BF16)
    p['b_down'] = b_down[l][:, None, :]
    return p


def _pad_time(a, t_pad):
    return a if a.shape[1] == t_pad else jnp.pad(a, ((0, 0), (0, t_pad - a.shape[1])) + ((0, 0),) * (a.ndim - 2))


def _trunk(x, cache, params, ln_in, moe_block):
    B, T, D = x.shape
    n = B * T
    past = 0 if cache is None else cache[0].shape[2]
    n_keys = past + T
    tkp = -(-n_keys // LANES) * LANES
    pos = jnp.arange(past, past + T, dtype=I32)
    tab_a = _rope_tables(pos, ROPE_A, ROPE_A)
    tab_c = _rope_tables(pos, ROT_C, HEAD_DIM)
    tab_i = _rope_tables(pos, ROT_IDX, IDX_DIM)
    tile_mat = (jnp.arange(ROPE_A)[:, None] == (jnp.arange(LANES)[None, :] % ROPE_A)).astype(BF16)
    tm = _tile(T, 512)
    key_major = T % LANES == 0 and tkp == n_keys

    h, hb = _ln_in(x.reshape(n, D), *ln_in)
    rows = [[] for _ in range(8)]
    for l, p in enumerate(params):
        hb3 = hb.reshape(B, T, D)
        qn, qr, ckv_new, kr_new = _proj_call(
            _mla_proj_kernel, hb3, [p['w_mla'], p['w_uq'], p['g_qa'], p['g_kva']], tab_a,
            [(BRANCH_W, BF16), (2 * LANES, BF16), (KV_LORA, F32), (ROPE_A, F32)], tm)
        qb, kb_new, vb_new, kb16, vb16, lf_new = _proj_call(
            _fox_proj_kernel, hb3, [p['w_fox'], p['b_f']], (),
            [(BRANCH_W, BF16), (BRANCH_W, F32), (BRANCH_W, F32), (BRANCH_W, BF16), (BRANCH_W, BF16),
             (N_HEADS, F32)], tm)
        qc, kc_new, vc_new, kc16, vc16, iq, ik_new, iw = _proj_call(
            _dsa_proj_kernel, hb3, [p['w_dsa']], tab_c + tab_i,
            [(BRANCH_W, BF16), (BRANCH_W, F32), (BRANCH_W, F32), (BRANCH_W, BF16), (BRANCH_W, BF16),
             (IDX_H * IDX_DIM, BF16), (IDX_DIM, F32), (IDX_H, F32)], tm)

        if cache is None:
            ckv_all, kr_all, lf_all, ik_all = ckv_new, kr_new, lf_new, ik_new
            kb_all, vb_all, kc_all, vc_all = kb16, vb16, kc16, vc16
        else:
            c_ckv, c_kr, c_kb, c_vb, c_lf, c_kc, c_vc, c_ik = (c[l] for c in cache)
            flat = lambda c: c.reshape(B, past, BRANCH_W).astype(BF16)
            ckv_all = jnp.concatenate([c_ckv, ckv_new], axis=1)
            kr_all = jnp.concatenate([c_kr, kr_new], axis=1)
            lf_all = jnp.concatenate([c_lf, lf_new], axis=1)
            ik_all = jnp.concatenate([c_ik, ik_new], axis=1)
            kb_all = jnp.concatenate([flat(c_kb), kb16], axis=1)
            vb_all = jnp.concatenate([flat(c_vb), vb16], axis=1)
            kc_all = jnp.concatenate([flat(c_kc), kc16], axis=1)
            vc_all = jnp.concatenate([flat(c_vc), vc16], axis=1)
        ckv_all, kr_all, lf_all, ik_all, kb_all, vb_all, kc_all, vc_all = (
            _pad_time(a, tkp) for a in (ckv_all, kr_all, lf_all, ik_all, kb_all, vb_all, kc_all, vc_all))

        kpp, v_a = _mla_kv(ckv_all, kr_all, p['w_ukv'], tile_mat)
        cum = _cumsum_time(jnp.swapaxes(lf_all, 1, 2))
        if key_major:
            o_a = _mla_attention_t(qn, qr, kpp, v_a, past, n_keys)
            o_b = _fox_attention_t(qb, kb_all, vb_all, cum[:, :, past:past + T], jnp.swapaxes(cum, 1, 2), past)
            ik4 = jnp.tile(ik_all.astype(BF16), (1, 1, LANES // IDX_DIM))
            o_c = _dsa_attention_t(qc, kc_all, vc_all, iq, ik4, jnp.swapaxes(iw, 1, 2), past, n_keys)
        else:
            o_a = _mla_attention(qn, qr, kpp, v_a, past, n_keys)
            cq = jnp.swapaxes(cum[:, :, past:past + T], 1, 2)
            o_b = _fox_attention(qb, kb_all, vb_all, cq, cum, past)
            tk = _tile(tkp, 512)
            ikt = jnp.swapaxes(ik_all.astype(BF16), 1, 2)
            ikt = jnp.tile(ikt, (1, LANES // IDX_DIM, 1))
            ikt = jnp.swapaxes(ikt.reshape(B, LANES, tkp // tk, tk), 1, 2)
            o_c = _dsa_attention(qc, kc_all, vc_all, iq, ikt, iw, past, n_keys)

        x1, x1b, top_idx, gate = _merge(
            h, hb, o_a.reshape(n, -1), o_b.reshape(n, -1), o_c.reshape(n, -1),
            p['w_gate'], p['b_gate'], p['w_br'], p['w_out'], *p['ln1'], p['w_router'], p['b_router'])

        blk_e, idx, n_rows = _dispatch_tables(top_idx, moe_block)
        y4 = _experts(x1, blk_e, idx, p['w_up'], p['b_up'], p['w_down'], p['b_down'], n_rows, moe_block)
        h, hb = _combine(x1, y4, gate, *p['ln2'])

        new = (ckv_new, kr_new, kb_new.reshape(B, T, N_HEADS, HEAD_DIM), vb_new.reshape(B, T, N_HEADS, HEAD_DIM),
               lf_new, kc_new.reshape(B, T, N_HEADS, HEAD_DIM), vc_new.reshape(B, T, N_HEADS, HEAD_DIM), ik_new)
        for acc, r in zip(rows, new):
            acc.append(r)
    return h.reshape(B, T, D), [jnp.stack(a) for a in rows]


def kernel(x_prompt, x_sample, cache_mla_ckv, cache_mla_krope, cache_fox_k, cache_fox_v, cache_fox_logf,
           cache_dsa_k, cache_dsa_v, cache_dsa_idxk, ln_in_g, ln_in_b, w_in, b_f, b_gate, g_qa, g_kva,
           w_uq, w_ukv, w_br, w_out, ln1_g, ln1_b, w_router, b_router, w_up, b_up, w_down, b_down,
           ln2_g, ln2_b):
    depth = w_in.shape[0]
    w_up = _deinterleave_up(w_up)
    params = [_prep_layer(l, w_in, b_f, b_gate, g_qa, g_kva, w_uq, w_ukv, w_br, w_out, ln1_g, ln1_b,
                          w_router, b_router, w_up, b_up, w_down, b_down, ln2_g, ln2_b) for l in range(depth)]
    ln_in = (ln_in_g, ln_in_b)
    y_p, st_p = _trunk(x_prompt, None, params, ln_in, moe_block=512)
    caches = (cache_mla_ckv, cache_mla_krope, cache_fox_k, cache_fox_v, cache_fox_logf,
              cache_dsa_k, cache_dsa_v, cache_dsa_idxk)
    y_s, st_s = _trunk(x_sample, caches, params, ln_in, moe_block=128)
    out = [y_p, y_s]
    for a, b in zip(st_p, st_s):
        out += [a, b]
    return tuple(out)
```

```python
import functools

import jax
import jax.numpy as jnp
from jax import lax
from jax.experimental import pallas as pl
from jax.experimental.pallas import tpu as pltpu

F32 = jnp.float32
BF16 = jnp.bfloat16
I32 = jnp.int32

D_MODEL = 1024
CHUNK = 64
CHUNK_SHIFT = 6
ROPE_THETA = 500000.0
LN_EPS = 1e-5
RMS_EPS = 1e-6
DEPTH_NOMINAL = 4
DN_ALPHA = (2 * DEPTH_NOMINAL) ** 0.25
N_HEADS = 8
HEAD_DIM = 64
Q_LORA = 384
KV_LORA = 256
ROPE_A = 32
ROT_C = 16
IDX_H = 8
IDX_DIM = 32
ROT_IDX = 8
TOPK_MAX = 256
N_BRANCH = 3
BRANCH_W = N_HEADS * HEAD_DIM
N_EXPERTS = 32
TOP_K = 4
D_FF = 1024
SWIGLU_LIMIT = 7.0
SWIGLU_ALPHA = 1.702

LANES = 128
VMEM_LIMIT_BYTES = 56 * 2**20

LOG2_E = 1.4426950408889634
MLA_Q_SCALE = (HEAD_DIM + ROPE_A) ** -0.5 * LOG2_E
NEG = -1e30
INT_MIN = -2**31
KEY_NEG_INF = -2139095041
BIG_IDX = 2**30


def _cparams(*sem):
    return pltpu.CompilerParams(dimension_semantics=sem, vmem_limit_bytes=VMEM_LIMIT_BYTES)


def _tile(n, pref):
    return pref if n % pref == 0 else n


def _full(shape):
    zeros = (0,) * len(shape)
    return pl.BlockSpec(shape, lambda *_: zeros)


def _dot(a, b):
    return jnp.dot(a, b, preferred_element_type=F32)


def _dot_nt(a, b):
    return lax.dot_general(a, b, (((1,), (1,)), ((), ())), preferred_element_type=F32)


def _layer_norm(x, g, b):
    xc = x - jnp.mean(x, axis=-1, keepdims=True)
    var = jnp.mean(xc * xc, axis=-1, keepdims=True)
    return xc * lax.rsqrt(var + LN_EPS) * g + b


def _rms_norm(x, g):
    return x * lax.rsqrt(jnp.mean(x * x, axis=-1, keepdims=True) + RMS_EPS) * g


def _rope_slab(x, c, sa, sb, half):
    return x * c + pltpu.roll(x, half, 1) * sa + pltpu.roll(x, LANES - half, 1) * sb


def _rope_tables(pos, rot, period):
    half = rot // 2
    inv_freq = ROPE_THETA ** (-jnp.arange(half, dtype=F32) / half)
    ang = pos.astype(F32)[:, None] * inv_freq
    j = jnp.arange(LANES) % period
    cos = jnp.cos(ang)[:, j % half]
    sin = jnp.sin(ang)[:, j % half]
    c = jnp.where(j < rot, cos, 1.0)
    sa = jnp.where((j >= half) & (j < rot), sin, 0.0)
    sb = jnp.where(j < half, -sin, 0.0)
    return c, sa, sb


def _ln_kernel(x_ref, g_ref, b_ref, o_ref, ob_ref):
    y = _layer_norm(x_ref[...], g_ref[...], b_ref[...])
    o_ref[...] = y
    ob_ref[...] = y.astype(BF16)


def _ln_in(x, g, b):
    n, d = x.shape
    tm = _tile(n, 512)
    row = pl.BlockSpec((tm, d), lambda i: (i, 0))
    return pl.pallas_call(
        _ln_kernel, grid=(n // tm,),
        in_specs=[row, _full((1, d)), _full((1, d))],
        out_specs=[row, row],
        out_shape=[jax.ShapeDtypeStruct((n, d), F32), jax.ShapeDtypeStruct((n, d), BF16)],
        compiler_params=_cparams("parallel"),
    )(x, g.reshape(1, d), b.reshape(1, d))


def _mla_proj_kernel(hb_ref, w_ref, wuq_ref, gqa_ref, gkva_ref, c_ref, sa_ref, sb_ref,
                     qn_ref, qr_ref, ckv_ref, kr_ref):
    z = _dot(hb_ref[0], w_ref[...])
    qa = _rms_norm(z[:, :Q_LORA], gqa_ref[...])
    q = _dot(qa.astype(BF16), wuq_ref[...]) * MLA_Q_SCALE
    qn_ref[0] = q[:, :BRANCH_W].astype(BF16)
    c, sa, sb = c_ref[...], sa_ref[...], sb_ref[...]
    for s in range(2):
        lo = BRANCH_W + s * LANES
        qr_ref[0, :, s * LANES:(s + 1) * LANES] = _rope_slab(q[:, lo:lo + LANES], c, sa, sb, ROPE_A // 2).astype(BF16)
    ckv_ref[0] = _rms_norm(z[:, Q_LORA:Q_LORA + KV_LORA], gkva_ref[...])
    kr = _rope_slab(z[:, Q_LORA + KV_LORA:], c, sa, sb, ROPE_A // 2)
    kr_ref[0] = kr[:, :ROPE_A]


def _fox_proj_kernel(hb_ref, w_ref, bf_ref, q_ref, k_ref, v_ref, k16_ref, v16_ref, lf_ref):
    z = _dot(hb_ref[0], w_ref[...])
    w = BRANCH_W
    q_ref[0] = (z[:, :w] * HEAD_DIM ** -0.5).astype(BF16)
    k = z[:, w:2 * w]
    v = z[:, 2 * w:3 * w]
    k_ref[0] = k
    v_ref[0] = v
    k16_ref[0] = k.astype(BF16)
    v16_ref[0] = v.astype(BF16)
    x = z[:, 3 * w:3 * w + N_HEADS] + bf_ref[...]
    lf_ref[0] = jnp.minimum(x, 0.0) - jnp.log1p(jnp.exp(-jnp.abs(x)))


def _dsa_proj_kernel(hb_ref, w_ref, c64_ref, sa64_ref, sb64_ref, c32_ref, sa32_ref, sb32_ref,
                     q_ref, k_ref, v_ref, k16_ref, v16_ref, iq_ref, ik_ref, iw_ref):
    z = _dot(hb_ref[0], w_ref[...])
    w = BRANCH_W
    c64, sa64, sb64 = c64_ref[...], sa64_ref[...], sb64_ref[...]
    c32, sa32, sb32 = c32_ref[...], sa32_ref[...], sb32_ref[...]
    for s in range(w // LANES):
        sl = slice(s * LANES, (s + 1) * LANES)
        qs = _rope_slab(z[:, s * LANES:(s + 1) * LANES], c64, sa64, sb64, ROT_C // 2)
        q_ref[0, :, sl] = (qs * HEAD_DIM ** -0.5).astype(BF16)
        ks = _rope_slab(z[:, w + s * LANES:w + (s + 1) * LANES], c64, sa64, sb64, ROT_C // 2)
        k_ref[0, :, sl] = ks
        k16_ref[0, :, sl] = ks.astype(BF16)
    v = z[:, 2 * w:3 * w]
    v_ref[0] = v
    v16_ref[0] = v.astype(BF16)
    o = 3 * w
    for s in range(2):
        iq_ref[0, :, s * LANES:(s + 1) * LANES] = _rope_slab(
            z[:, o + s * LANES:o + (s + 1) * LANES], c32, sa32, sb32, ROT_IDX // 2).astype(BF16)
    o += IDX_H * IDX_DIM
    ik = _rope_slab(z[:, o:o + LANES], c32, sa32, sb32, ROT_IDX // 2)
    ik_ref[0] = ik[:, :IDX_DIM]
    o += LANES
    iw_ref[0] = z[:, o:o + IDX_H] * (IDX_DIM ** -0.5 * IDX_H ** -0.5)


def _proj_call(kernel, hb, weights, tables, out_widths_dtypes, tm):
    B, T, D = hb.shape
    grid = (B, T // tm)
    in_specs = [pl.BlockSpec((1, tm, D), lambda b, t: (b, t, 0))]
    in_specs += [_full(w.shape) for w in weights]
    in_specs += [pl.BlockSpec((tm, LANES), lambda b, t: (t, 0)) for _ in tables]
    out_specs = [pl.BlockSpec((1, tm, w), lambda b, t: (b, t, 0)) for w, _ in out_widths_dtypes]
    out_shape = [jax.ShapeDtypeStruct((B, T, w), dt) for w, dt in out_widths_dtypes]
    return pl.pallas_call(kernel, grid=grid, in_specs=in_specs, out_specs=out_specs, out_shape=out_shape,
                          compiler_params=_cparams("parallel", "parallel"))(hb, *weights, *tables)


def _mla_kv_kernel(ckv_ref, kr_ref, w_ref, tile_ref, kpp_ref, v_ref):
    kv = _dot(ckv_ref[0].astype(BF16), w_ref[...])
    kr4 = _dot(kr_ref[0].astype(BF16), tile_ref[...]).astype(BF16)
    for p in range(N_HEADS // 2):
        kpp_ref[0, p, :, :LANES] = kv[:, p * LANES:(p + 1) * LANES].astype(BF16)
        kpp_ref[0, p, :, LANES:] = kr4
    v_ref[0] = kv[:, BRANCH_W:].astype(BF16)


def _mla_kv(ckv, kr, w_ukv_p, tile_mat):
    B, Tk, _ = ckv.shape
    tm = _tile(Tk, 512)
    return pl.pallas_call(
        _mla_kv_kernel, grid=(B, Tk // tm),
        in_specs=[pl.BlockSpec((1, tm, KV_LORA), lambda b, t: (b, t, 0)),
                  pl.BlockSpec((1, tm, ROPE_A), lambda b, t: (b, t, 0)),
                  _full(w_ukv_p.shape), _full(tile_mat.shape)],
        out_specs=[pl.BlockSpec((1, N_HEADS // 2, tm, 2 * LANES), lambda b, t: (b, 0, t, 0)),
                   pl.BlockSpec((1, tm, BRANCH_W), lambda b, t: (b, t, 0))],
        out_shape=[jax.ShapeDtypeStruct((B, N_HEADS // 2, Tk, 2 * LANES), BF16),
                   jax.ShapeDtypeStruct((B, Tk, BRANCH_W), BF16)],
        compiler_params=_cparams("parallel", "parallel"),
    )(ckv, kr, w_ukv_p, tile_mat)


def _cumsum_kernel(x_ref, u_ref, o_ref, carry_ref):
    @pl.when(pl.program_id(1) == 0)
    def _():
        carry_ref[...] = jnp.zeros_like(carry_ref)

    x = x_ref[0]
    u = u_ref[...]
    hi = x.astype(BF16)
    r1 = x - hi.astype(F32)
    mid = r1.astype(BF16)
    lo = (r1 - mid.astype(F32)).astype(BF16)
    out = _dot(hi, u) + _dot(mid, u) + _dot(lo, u) + carry_ref[...]
    o_ref[0] = out
    tc = x.shape[1]
    carry_ref[...] = out[:, tc - 1:tc]


def _cumsum_time(x):
    B, H, Tk = x.shape
    tc = _tile(Tk, 512)
    u = (jnp.arange(tc)[:, None] <= jnp.arange(tc)[None, :]).astype(BF16)
    return pl.pallas_call(
        _cumsum_kernel, grid=(B, Tk // tc),
        in_specs=[pl.BlockSpec((1, H, tc), lambda b, t: (b, 0, t)), _full((tc, tc))],
        out_specs=pl.BlockSpec((1, H, tc), lambda b, t: (b, 0, t)),
        out_shape=jax.ShapeDtypeStruct((B, H, Tk), F32),
        scratch_shapes=[pltpu.VMEM((H, 1), F32)],
        compiler_params=_cparams("parallel", "arbitrary"),
    )(x, u)


def _own_lanes(lane, h, width):
    per = LANES // width
    return jnp.right_shift(lane, width.bit_length() - 1) == (h % per)


def _keep_head(x, lane, h, width):
    return jnp.where(_own_lanes(lane, h, width), x.astype(F32), 0.0).astype(BF16)


def _softmax_step(h, s, v_slab, m_ref, l_ref, acc_ref, exp=jnp.exp):
    m_prev = m_ref[h]
    m_new = jnp.maximum(m_prev, jnp.max(s, axis=1, keepdims=True))
    alpha = exp(m_prev - m_new)
    p = exp(s - m_new)
    l_ref[h] = alpha * l_ref[h] + jnp.sum(p, axis=1, keepdims=True)
    acc_ref[h] = alpha * acc_ref[h] + _dot(p.astype(BF16), v_slab)
    m_ref[h] = m_new


def _init_softmax(m_ref, l_ref, acc_ref):
    m_ref[...] = jnp.full(m_ref.shape, NEG, F32)
    l_ref[...] = jnp.zeros(l_ref.shape, F32)
    acc_ref[...] = jnp.zeros(acc_ref.shape, F32)


def _write_heads(o_ref, l_ref, acc_ref, lane):
    for p in range(N_HEADS // 2):
        a = acc_ref[2 * p] * (1.0 / l_ref[2 * p])
        b = acc_ref[2 * p + 1] * (1.0 / l_ref[2 * p + 1])
        o_ref[0, :, p * LANES:(p + 1) * LANES] = jnp.where(lane < HEAD_DIM, a, b).astype(BF16)


def _softmax_scratch(tq):
    return [pltpu.VMEM((N_HEADS, tq, 1), F32), pltpu.VMEM((N_HEADS, tq, 1), F32),
            pltpu.VMEM((N_HEADS, tq, LANES), F32)]


def _mla_attn_kernel(qn_ref, qr_ref, kpp_ref, v_ref, o_ref, qs_ref, m_ref, l_ref, acc_ref,
                     *, tq, tk, past, n_keys, padded):
    qi = pl.program_id(1)
    ki = pl.program_id(2)
    q0 = past + qi * tq
    last_k = (jnp.minimum(n_keys, ((q0 + tq - 1) // CHUNK + 1) * CHUNK) - 1) // tk
    lane = lax.broadcasted_iota(I32, (tq, LANES), 1)

    @pl.when(ki == 0)
    def _init():
        _init_softmax(m_ref, l_ref, acc_ref)
        for h in range(N_HEADS):
            qn = qn_ref[0, :, (h // 2) * LANES:(h // 2 + 1) * LANES]
            qr = qr_ref[0, :, (h // 4) * LANES:(h // 4 + 1) * LANES]
            qs_ref[h, :, :LANES] = _keep_head(qn, lane, h, HEAD_DIM)
            qs_ref[h, :, LANES:] = _keep_head(qr, lane, h, ROPE_A)

    @pl.when(ki <= last_k)
    def _step():
        qpos = q0 + lax.broadcasted_iota(I32, (tq, tk), 0)
        kpos = ki * tk + lax.broadcasted_iota(I32, (tq, tk), 1)
        kchunk = jnp.right_shift(kpos, CHUNK_SHIFT)
        if padded:
            kchunk = jnp.where(kpos < n_keys, kchunk, BIG_IDX)
        ok = kchunk <= jnp.right_shift(qpos, CHUNK_SHIFT)
        for h in range(N_HEADS):
            p = h // 2
            s = jnp.where(ok, _dot_nt(qs_ref[h], kpp_ref[0, p]), NEG)
            _softmax_step(h, s, v_ref[0, :, p * LANES:(p + 1) * LANES], m_ref, l_ref, acc_ref, exp=jnp.exp2)

    @pl.when(ki == pl.num_programs(2) - 1)
    def _fin():
        _write_heads(o_ref, l_ref, acc_ref, lane)


def _mla_attention(qn, qr, kpp, v, past, n_keys):
    B, T, _ = qn.shape
    Tkp = v.shape[1]
    tq = _tile(T, 512)
    tk = _tile(Tkp, 512)

    def kblk(qi, ki):
        last = (jnp.minimum(n_keys, ((past + qi * tq + tq - 1) // CHUNK + 1) * CHUNK) - 1) // tk
        return jnp.minimum(ki, last)

    kern = functools.partial(_mla_attn_kernel, tq=tq, tk=tk, past=past, n_keys=n_keys, padded=Tkp != n_keys)
    return pl.pallas_call(
        kern, grid=(B, T // tq, Tkp // tk),
        in_specs=[pl.BlockSpec((1, tq, BRANCH_W), lambda b, qi, ki: (b, qi, 0)),
                  pl.BlockSpec((1, tq, 2 * LANES), lambda b, qi, ki: (b, qi, 0)),
                  pl.BlockSpec((1, N_HEADS // 2, tk, 2 * LANES), lambda b, qi, ki: (b, 0, kblk(qi, ki), 0)),
                  pl.BlockSpec((1, tk, BRANCH_W), lambda b, qi, ki: (b, kblk(qi, ki), 0))],
        out_specs=pl.BlockSpec((1, tq, BRANCH_W), lambda b, qi, ki: (b, qi, 0)),
        out_shape=jax.ShapeDtypeStruct((B, T, BRANCH_W), BF16),
        scratch_shapes=[pltpu.VMEM((N_HEADS, tq, 2 * LANES), BF16)] + _softmax_scratch(tq),
        compiler_params=_cparams("parallel", "parallel", "arbitrary"),
    )(qn, qr, kpp, v)


def _fox_attn_kernel(q_ref, k_ref, v_ref, cq_ref, ck_ref, o_ref, qs_ref, m_ref, l_ref, acc_ref,
                     *, tq, tk, past):
    qi = pl.program_id(1)
    ki = pl.program_id(2)
    q0 = past + qi * tq
    last_k = (q0 + tq - 1) // tk
    lane = lax.broadcasted_iota(I32, (tq, LANES), 1)

    @pl.when(ki == 0)
    def _init():
        _init_softmax(m_ref, l_ref, acc_ref)
        for h in range(N_HEADS):
            q = q_ref[0, :, (h // 2) * LANES:(h // 2 + 1) * LANES]
            qs_ref[h] = _keep_head(q, lane, h, HEAD_DIM)

    @pl.when(ki <= last_k)
    def _step():
        qpos = q0 + lax.broadcasted_iota(I32, (tq, tk), 0)
        kpos = ki * tk + lax.broadcasted_iota(I32, (tq, tk), 1)
        ok = kpos <= qpos
        for h in range(N_HEADS):
            p = h // 2
            s = _dot_nt(qs_ref[h], k_ref[0, :, p * LANES:(p + 1) * LANES])
            s = s + cq_ref[0, :, h:h + 1] - ck_ref[0, h:h + 1, :]
            s = jnp.where(ok, s, NEG)
            _softmax_step(h, s, v_ref[0, :, p * LANES:(p + 1) * LANES], m_ref, l_ref, acc_ref)

    @pl.when(ki == pl.num_programs(2) - 1)
    def _fin():
        _write_heads(o_ref, l_ref, acc_ref, lane)


def _fox_attention(q, k, v, cq, ck, past):
    B, T, _ = q.shape
    Tkp = k.shape[1]
    tq = _tile(T, 512)
    tk = _tile(Tkp, 512)

    def kblk(qi, ki):
        return jnp.minimum(ki, (past + qi * tq + tq - 1) // tk)

    kern = functools.partial(_fox_attn_kernel, tq=tq, tk=tk, past=past)
    return pl.pallas_call(
        kern, grid=(B, T // tq, Tkp // tk),
        in_specs=[pl.BlockSpec((1, tq, BRANCH_W), lambda b, qi, ki: (b, qi, 0)),
                  pl.BlockSpec((1, tk, BRANCH_W), lambda b, qi, ki: (b, kblk(qi, ki), 0)),
                  pl.BlockSpec((1, tk, BRANCH_W), lambda b, qi, ki: (b, kblk(qi, ki), 0)),
                  pl.BlockSpec((1, tq, N_HEADS), lambda b, qi, ki: (b, qi, 0)),
                  pl.BlockSpec((1, N_HEADS, tk), lambda b, qi, ki: (b, 0, kblk(qi, ki)))],
        out_specs=pl.BlockSpec((1, tq, BRANCH_W), lambda b, qi, ki: (b, qi, 0)),
        out_shape=jax.ShapeDtypeStruct((B, T, BRANCH_W), BF16),
        scratch_shapes=[pltpu.VMEM((N_HEADS, tq, LANES), BF16)] + _softmax_scratch(tq),
        compiler_params=_cparams("parallel", "parallel", "arbitrary"),
    )(q, k, v, cq, ck)


def _dsa_attn_kernel(q_ref, k_ref, v_ref, iq_ref, ikt_ref, iw_ref, o_ref,
                     key_ref, qs_ref, iqs_ref, m_ref, l_ref, acc_ref, cidx_ref,
                     *, tq, tk, past, n_keys, padded, n_keep, idx_bits):
    qi = pl.program_id(1)
    q0 = past + qi * tq
    adm_end = jnp.minimum(n_keys, ((q0 + tq - 1) // CHUNK + 1) * CHUNK)
    nkb = (adm_end + tk - 1) // tk
    lane = lax.broadcasted_iota(I32, (tq, LANES), 1)
    qpos = q0 + lax.broadcasted_iota(I32, (tq, tk), 0)
    col = lax.broadcasted_iota(I32, (tq, tk), 1)

    _init_softmax(m_ref, l_ref, acc_ref)
    for h in range(N_HEADS):
        q = q_ref[0, :, (h // 2) * LANES:(h // 2 + 1) * LANES]
        qs_ref[h] = _keep_head(q, lane, h, HEAD_DIM)
        iq = iq_ref[0, :, (h // 4) * LANES:(h // 4 + 1) * LANES]
        iqs_ref[h] = _keep_head(iq, lane, h, IDX_DIM)

    def score_body(kb, carry):
        ikt = ikt_ref[0, kb]
        sc = jnp.zeros((tq, tk), F32)
        for h in range(IDX_H):
            sc = sc + iw_ref[0, :, h:h + 1] * jnp.maximum(_dot(iqs_ref[h], ikt), 0.0)
        kpos = kb * tk + col
        bits = pltpu.bitcast(sc + 0.0, I32)
        okey = jnp.where(bits < 0, bits ^ 0x7FFFFFFF, bits)
        kchunk = jnp.right_shift(kpos, CHUNK_SHIFT)
        if padded:
            kchunk = jnp.where(kpos < n_keys, kchunk, BIG_IDX)
        adm = kchunk <= jnp.right_shift(qpos, CHUNK_SHIFT)
        key_ref[kb] = jnp.where(adm, okey, KEY_NEG_INF)
        return carry

    lax.fori_loop(0, nkb, score_body, 0)

    def count(pred):
        def body(kb, c):
            return c + jnp.sum(pred(key_ref[kb], kb), axis=1, keepdims=True)
        return lax.fori_loop(0, nkb, body, jnp.zeros((tq, 1), F32))

    def ones_where(cond):
        return jnp.where(cond, 1.0, 0.0)

    keep = float(n_keep)
    c_nonneg = count(lambda kt, kb: ones_where(kt >= 0))
    base = jnp.where(c_nonneg >= keep, jnp.zeros((tq, 1), I32), jnp.full((tq, 1), INT_MIN, I32))

    def bit_body(j, base):
        cand = base | jnp.left_shift(jnp.int32(1), 30 - j)
        c = count(lambda kt, kb: ones_where(kt >= cand))
        return jnp.where(c >= keep, cand, base)

    thr = lax.fori_loop(0, 31, bit_body, base)

    c_gt = count(lambda kt, kb: ones_where(kt > thr))
    c_ge = count(lambda kt, kb: ones_where(kt >= thr))
    need = keep - c_gt
    few = thr == KEY_NEG_INF
    tied = jnp.where(few, 0.0, ones_where(c_ge > keep))
    cidx_ref[...] = jnp.where(few, -1, BIG_IDX)

    @pl.when(jnp.max(tied) > 0.5)
    def _ties():
        def idx_body(j, b):
            cand = b | jnp.left_shift(jnp.int32(1), idx_bits - 1 - j)
            f = count(lambda kt, kb: jnp.where(kt == thr, ones_where(kb * tk + col < cand), 0.0))
            return jnp.where(f < need, cand, b)
        b = lax.fori_loop(0, idx_bits, idx_body, jnp.zeros((tq, 1), I32))
        cidx_ref[...] = jnp.where(few, -1, jnp.where(tied > 0.5, b, BIG_IDX))

    cidx = cidx_ref[...]

    def attn_body(kb, carry):
        k0 = pl.multiple_of(kb * tk, tk)
        kt = key_ref[kb]
        kpos = kb * tk + col
        bias = jnp.where(kt > thr, 0.0, jnp.where(kt == thr, jnp.where(kpos <= cidx, 0.0, NEG), NEG))
        for h in range(N_HEADS):
            p = h // 2
            s = _dot_nt(qs_ref[h], k_ref[0, pl.ds(k0, tk), p * LANES:(p + 1) * LANES]) + bias
            _softmax_step(h, s, v_ref[0, pl.ds(k0, tk), p * LANES:(p + 1) * LANES], m_ref, l_ref, acc_ref)
        return carry

    lax.fori_loop(0, nkb, attn_body, 0)
    _write_heads(o_ref, l_ref, acc_ref, lane)


def _dsa_attention(q, k, v, iq, ikt, iw, past, n_keys):
    B, T, _ = q.shape
    Tkp = k.shape[1]
    tq = _tile(T, 512)
    tk = _tile(Tkp, 512)
    nkb = Tkp // tk
    n_keep = min(TOPK_MAX, n_keys // 4)
    idx_bits = max(1, (Tkp - 1).bit_length())
    kern = functools.partial(_dsa_attn_kernel, tq=tq, tk=tk, past=past, n_keys=n_keys, padded=Tkp != n_keys,
                             n_keep=n_keep, idx_bits=idx_bits)
    return pl.pallas_call(
        kern, grid=(B, T // tq),
        in_specs=[pl.BlockSpec((1, tq, BRANCH_W), lambda b, qi: (b, qi, 0)),
                  pl.BlockSpec((1, Tkp, BRANCH_W), lambda b, qi: (b, 0, 0)),
                  pl.BlockSpec((1, Tkp, BRANCH_W), lambda b, qi: (b, 0, 0)),
                  pl.BlockSpec((1, tq, IDX_H * IDX_DIM), lambda b, qi: (b, qi, 0)),
                  pl.BlockSpec((1, nkb, LANES, tk), lambda b, qi: (b, 0, 0, 0)),
                  pl.BlockSpec((1, tq, IDX_H), lambda b, qi: (b, qi, 0))],
        out_specs=pl.BlockSpec((1, tq, BRANCH_W), lambda b, qi: (b, qi, 0)),
        out_shape=jax.ShapeDtypeStruct((B, T, BRANCH_W), BF16),
        scratch_shapes=[pltpu.VMEM((nkb, tq, tk), I32),
                        pltpu.VMEM((N_HEADS, tq, LANES), BF16),
                        pltpu.VMEM((IDX_H, tq, LANES), BF16)] + _softmax_scratch(tq)
                       + [pltpu.VMEM((tq, 1), I32)],
        compiler_params=_cparams("parallel", "arbitrary"),
    )(q, k, v, iq, ikt, iw)


def _heads_t(x_ref, qt_ref, lane, width, row0=0):
    per = LANES // width
    for h in range(N_HEADS):
        s = h // per
        x = x_ref[0, :, s * LANES:(s + 1) * LANES].astype(F32)
        x = jnp.where(_own_lanes(lane, h, width), x, 0.0)
        qt_ref[h, row0:row0 + LANES, :] = x.T.astype(BF16)


def _softmax_step_t(h, s, vt_h, m_ref, acc_ref, exp=jnp.exp):
    m_prev = m_ref[h]
    m_new = jnp.maximum(m_prev, jnp.max(s, axis=0, keepdims=True))
    alpha = exp(m_prev - m_new)
    p = exp(s - m_new)
    acc_ref[h] = alpha * acc_ref[h] + _dot(vt_h, p.astype(BF16))
    m_ref[h] = m_new


def _init_softmax_t(m_ref, acc_ref):
    m_ref[...] = jnp.full(m_ref.shape, NEG, F32)
    acc_ref[...] = jnp.zeros(acc_ref.shape, F32)


def _write_heads_t(o_ref, acc_ref):
    for p in range(N_HEADS // 2):
        halves = []
        for h in (2 * p, 2 * p + 1):
            a = acc_ref[h]
            halves.append(a[:HEAD_DIM] * (1.0 / a[HEAD_DIM:HEAD_DIM + 1]))
        o_ref[0, :, p * LANES:(p + 1) * LANES] = jnp.concatenate(halves, axis=0).T.astype(BF16)


def _softmax_scratch_t(tq):
    return [pltpu.VMEM((N_HEADS, 1, tq), F32), pltpu.VMEM((N_HEADS, LANES, tq), F32)]


def _mla_attn_t_kernel(qn_ref, qr_ref, kpp_ref, vt_ref, o_ref, qt_ref, m_ref, acc_ref,
                       *, tq, tk, past, n_keys):
    qi = pl.program_id(1)
    ki = pl.program_id(2)
    q0 = past + qi * tq
    last_k = (jnp.minimum(n_keys, ((q0 + tq - 1) // CHUNK + 1) * CHUNK) - 1) // tk

    @pl.when(ki == 0)
    def _init():
        _init_softmax_t(m_ref, acc_ref)
        lane = lax.broadcasted_iota(I32, (tq, LANES), 1)
        _heads_t(qn_ref, qt_ref, lane, HEAD_DIM)
        _heads_t(qr_ref, qt_ref, lane, ROPE_A, row0=LANES)

    interior = (ki * tk + tk - 1) // CHUNK <= q0 // CHUNK

    def block(masked):
        if masked:
            kpos = ki * tk + lax.broadcasted_iota(I32, (tk, tq), 0)
            qpos = q0 + lax.broadcasted_iota(I32, (tk, tq), 1)
            ok = jnp.right_shift(kpos, CHUNK_SHIFT) <= jnp.right_shift(qpos, CHUNK_SHIFT)
        for h in range(N_HEADS):
            p = h // 2
            s = _dot(kpp_ref[0, p], qt_ref[h])
            if masked:
                s = jnp.where(ok, s, NEG)
            _softmax_step_t(h, s, vt_ref[0, 0, h], m_ref, acc_ref, exp=jnp.exp2)

    @pl.when(jnp.logical_and(ki <= last_k, interior))
    def _interior():
        block(False)

    @pl.when(jnp.logical_and(ki <= last_k, jnp.logical_not(interior)))
    def _diagonal():
        block(True)

    @pl.when(ki == pl.num_programs(2) - 1)
    def _fin():
        _write_heads_t(o_ref, acc_ref)


def _fox_attn_t_kernel(q_ref, k_ref, vt_ref, cq_ref, ck_ref, o_ref, qt_ref, m_ref, acc_ref,
                       *, tq, tk, past):
    qi = pl.program_id(1)
    ki = pl.program_id(2)
    q0 = past + qi * tq
    last_k = (q0 + tq - 1) // tk

    @pl.when(ki == 0)
    def _init():
        _init_softmax_t(m_ref, acc_ref)
        _heads_t(q_ref, qt_ref, lax.broadcasted_iota(I32, (tq, LANES), 1), HEAD_DIM)

    interior = ki * tk + tk - 1 <= q0

    def block(masked):
        if masked:
            kpos = ki * tk + lax.broadcasted_iota(I32, (tk, tq), 0)
            qpos = q0 + lax.broadcasted_iota(I32, (tk, tq), 1)
            ok = kpos <= qpos
        for h in range(N_HEADS):
            p = h // 2
            s = _dot(k_ref[0, :, p * LANES:(p + 1) * LANES], qt_ref[h])
            s = s + cq_ref[0, h:h + 1, :] - ck_ref[0, :, h:h + 1]
            if masked:
                s = jnp.where(ok, s, NEG)
            _softmax_step_t(h, s, vt_ref[0, 0, h], m_ref, acc_ref)

    @pl.when(jnp.logical_and(ki <= last_k, interior))
    def _interior():
        block(False)

    @pl.when(jnp.logical_and(ki <= last_k, jnp.logical_not(interior)))
    def _diagonal():
        block(True)

    @pl.when(ki == pl.num_programs(2) - 1)
    def _fin():
        _write_heads_t(o_ref, acc_ref)


def _blocked_t(v, tk):
    B, Tk, _ = v.shape
    vt = jnp.transpose(v.reshape(B, Tk // tk, tk, N_HEADS, HEAD_DIM), (0, 1, 3, 4, 2))
    return jnp.concatenate([vt, jnp.ones_like(vt)], axis=3)


def _mla_attention_t(qn, qr, kpp, v, past, n_keys):
    B, T, _ = qn.shape
    Tk = v.shape[1]
    tq = _tile(T, 512)
    tk = _tile(Tk, 512)

    def kblk(qi, ki):
        last = (jnp.minimum(n_keys, ((past + qi * tq + tq - 1) // CHUNK + 1) * CHUNK) - 1) // tk
        return jnp.minimum(ki, last)

    kern = functools.partial(_mla_attn_t_kernel, tq=tq, tk=tk, past=past, n_keys=n_keys)
    return pl.pallas_call(
        kern, grid=(B, T // tq, Tk // tk),
        in_specs=[pl.BlockSpec((1, tq, BRANCH_W), lambda b, qi, ki: (b, qi, 0)),
                  pl.BlockSpec((1, tq, 2 * LANES), lambda b, qi, ki: (b, qi, 0)),
                  pl.BlockSpec((1, N_HEADS // 2, tk, 2 * LANES), lambda b, qi, ki: (b, 0, kblk(qi, ki), 0)),
                  pl.BlockSpec((1, 1, N_HEADS, LANES, tk), lambda b, qi, ki: (b, kblk(qi, ki), 0, 0, 0))],
        out_specs=pl.BlockSpec((1, tq, BRANCH_W), lambda b, qi, ki: (b, qi, 0)),
        out_shape=jax.ShapeDtypeStruct((B, T, BRANCH_W), BF16),
        scratch_shapes=[pltpu.VMEM((N_HEADS, 2 * LANES, tq), BF16)] + _softmax_scratch_t(tq),
        compiler_params=_cparams("parallel", "parallel", "arbitrary"),
    )(qn, qr, kpp, _blocked_t(v, tk))


def _fox_attention_t(q, k, v, cq_t, ck, past):
    B, T, _ = q.shape
    Tk = k.shape[1]
    tq = _tile(T, 512)
    tk = _tile(Tk, 512)

    def kblk(qi, ki):
        return jnp.minimum(ki, (past + qi * tq + tq - 1) // tk)

    kern = functools.partial(_fox_attn_t_kernel, tq=tq, tk=tk, past=past)
    return pl.pallas_call(
        kern, grid=(B, T // tq, Tk // tk),
        in_specs=[pl.BlockSpec((1, tq, BRANCH_W), lambda b, qi, ki: (b, qi, 0)),
                  pl.BlockSpec((1, tk, BRANCH_W), lambda b, qi, ki: (b, kblk(qi, ki), 0)),
                  pl.BlockSpec((1, 1, N_HEADS, LANES, tk), lambda b, qi, ki: (b, kblk(qi, ki), 0, 0, 0)),
                  pl.BlockSpec((1, N_HEADS, tq), lambda b, qi, ki: (b, 0, qi)),
                  pl.BlockSpec((1, tk, N_HEADS), lambda b, qi, ki: (b, kblk(qi, ki), 0))],
        out_specs=pl.BlockSpec((1, tq, BRANCH_W), lambda b, qi, ki: (b, qi, 0)),
        out_shape=jax.ShapeDtypeStruct((B, T, BRANCH_W), BF16),
        scratch_shapes=[pltpu.VMEM((N_HEADS, LANES, tq), BF16)] + _softmax_scratch_t(tq),
        compiler_params=_cparams("parallel", "parallel", "arbitrary"),
    )(q, k, _blocked_t(v, tk), cq_t, ck)


def _dsa_attn_t_kernel(q_ref, k_ref, vt_ref, iq_ref, ik_ref, iw_ref, o_ref,
                       key_ref, qt_ref, iqt_ref, m_ref, acc_ref, cidx_ref,
                       *, tq, tk, past, n_keys, n_keep, idx_bits):
    qi = pl.program_id(1)
    q0 = past + qi * tq
    adm_end = jnp.minimum(n_keys, ((q0 + tq - 1) // CHUNK + 1) * CHUNK)
    nkb = (adm_end + tk - 1) // tk
    lane = lax.broadcasted_iota(I32, (tq, LANES), 1)
    qchunk = jnp.right_shift(q0 + lax.broadcasted_iota(I32, (tk, tq), 1), CHUNK_SHIFT)
    krow = lax.broadcasted_iota(I32, (tk, tq), 0)

    _init_softmax_t(m_ref, acc_ref)
    _heads_t(q_ref, qt_ref, lane, HEAD_DIM)
    _heads_t(iq_ref, iqt_ref, lane, IDX_DIM)

    def score_body(kb, carry):
        k0 = pl.multiple_of(kb * tk, tk)
        ik = ik_ref[0, pl.ds(k0, tk), :]
        sc = jnp.zeros((tk, tq), F32)
        for h in range(IDX_H):
            sc = sc + iw_ref[0, h:h + 1, :] * jnp.maximum(_dot(ik, iqt_ref[h]), 0.0)
        bits = pltpu.bitcast(sc + 0.0, I32)
        okey = jnp.where(bits < 0, bits ^ 0x7FFFFFFF, bits)
        adm = jnp.right_shift(k0 + krow, CHUNK_SHIFT) <= qchunk
        key_ref[kb] = jnp.where(adm, okey, KEY_NEG_INF)
        return carry

    lax.fori_loop(0, nkb, score_body, 0)

    def count(pred):
        def body(kb, c):
            return c + jnp.sum(pred(key_ref[kb], kb), axis=0, keepdims=True)
        return lax.fori_loop(0, nkb, body, jnp.zeros((1, tq), F32))

    def ones_where(cond):
        return jnp.where(cond, 1.0, 0.0)

    keep = float(n_keep)
    c_nonneg = count(lambda kt, kb: ones_where(kt >= 0))
    base = jnp.where(c_nonneg >= keep, jnp.zeros((1, tq), I32), jnp.full((1, tq), INT_MIN, I32))

    def bit_body(j, base):
        cand = base | jnp.left_shift(jnp.int32(1), 30 - j)
        c = count(lambda kt, kb: ones_where(kt >= cand))
        return jnp.where(c >= keep, cand, base)

    thr = lax.fori_loop(0, 31, bit_body, base)

    c_gt = count(lambda kt, kb: ones_where(kt > thr))
    c_ge = count(lambda kt, kb: ones_where(kt >= thr))
    need = keep - c_gt
    few = thr == KEY_NEG_INF
    tied = jnp.where(few, 0.0, ones_where(c_ge > keep))
    cidx_ref[...] = jnp.where(few, -1, BIG_IDX)

    @pl.when(jnp.max(tied) > 0.5)
    def _ties():
        def idx_body(j, b):
            cand = b | jnp.left_shift(jnp.int32(1), idx_bits - 1 - j)
            f = count(lambda kt, kb: jnp.where(kt == thr, ones_where(kb * tk + krow < cand), 0.0))
            return jnp.where(f < need, cand, b)
        b = lax.fori_loop(0, idx_bits, idx_body, jnp.zeros((1, tq), I32))
        cidx_ref[...] = jnp.where(few, -1, jnp.where(tied > 0.5, b, BIG_IDX))

    cidx = cidx_ref[...]

    def attn_body(kb, carry):
        k0 = pl.multiple_of(kb * tk, tk)
        kt = key_ref[kb]
        kpos = k0 + krow
        bias = jnp.where(kt > thr, 0.0, jnp.where(kt == thr, jnp.where(kpos <= cidx, 0.0, NEG), NEG))
        for h in range(N_HEADS):
            p = h // 2
            s = _dot(k_ref[0, pl.ds(k0, tk), p * LANES:(p + 1) * LANES], qt_ref[h]) + bias
            _softmax_step_t(h, s, vt_ref[0, kb, h], m_ref, acc_ref)
        return carry

    lax.fori_loop(0, nkb, attn_body, 0)
    _write_heads_t(o_ref, acc_ref)


def _dsa_attention_t(q, k, v, iq, ik4, iw_t, past, n_keys):
    B, T, _ = q.shape
    Tk = k.shape[1]
    tq = _tile(T, 512)
    tk = _tile(Tk, 512)
    nkb = Tk // tk
    n_keep = min(TOPK_MAX, n_keys // 4)
    idx_bits = max(1, (Tk - 1).bit_length())
    kern = functools.partial(_dsa_attn_t_kernel, tq=tq, tk=tk, past=past, n_keys=n_keys,
                             n_keep=n_keep, idx_bits=idx_bits)
    return pl.pallas_call(
        kern, grid=(B, T // tq),
        in_specs=[pl.BlockSpec((1, tq, BRANCH_W), lambda b, qi: (b, qi, 0)),
                  pl.BlockSpec((1, Tk, BRANCH_W), lambda b, qi: (b, 0, 0)),
                  pl.BlockSpec((1, nkb, N_HEADS, LANES, tk), lambda b, qi: (b, 0, 0, 0, 0)),
                  pl.BlockSpec((1, tq, IDX_H * IDX_DIM), lambda b, qi: (b, qi, 0)),
                  pl.BlockSpec((1, Tk, LANES), lambda b, qi: (b, 0, 0)),
                  pl.BlockSpec((1, IDX_H, tq), lambda b, qi: (b, 0, qi))],
        out_specs=pl.BlockSpec((1, tq, BRANCH_W), lambda b, qi: (b, qi, 0)),
        out_shape=jax.ShapeDtypeStruct((B, T, BRANCH_W), BF16),
        scratch_shapes=[pltpu.VMEM((nkb, tk, tq), I32),
                        pltpu.VMEM((N_HEADS, LANES, tq), BF16),
                        pltpu.VMEM((IDX_H, LANES, tq), BF16)] + _softmax_scratch_t(tq)
                       + [pltpu.VMEM((1, tq), I32)],
        compiler_params=_cparams("parallel", "arbitrary"),
    )(q, k, _blocked_t(v, tk), iq, ik4, iw_t)


def _merge_kernel(h_ref, hb_ref, oa_ref, ob_ref, oc_ref, wg_ref, bg_ref, wbr_ref, wout_ref, g_ref, b_ref,
                  wr_ref, br_ref, x_ref, xb_ref, idx_ref, gate_ref):
    hb = hb_ref[...]
    d = D_MODEL
    merged = None
    for n, o_ref in enumerate((oa_ref, ob_ref, oc_ref)):
        gt = _dot(hb, wg_ref[:, n * d:(n + 1) * d]) + bg_ref[:, n * d:(n + 1) * d]
        term = (1.0 / (1.0 + jnp.exp(-gt))) * _dot(o_ref[...], wbr_ref[n])
        merged = term if merged is None else merged + term
    out = _dot(merged.astype(BF16), wout_ref[...])
    x = _layer_norm(DN_ALPHA * h_ref[...] + out, g_ref[...], b_ref[...])
    xb = x.astype(BF16)
    x_ref[...] = x
    xb_ref[...] = xb

    logits = _dot(xb, wr_ref[...]) + br_ref[...]
    tm = logits.shape[0]
    lane = lax.broadcasted_iota(I32, (tm, N_EXPERTS), 1).astype(F32)
    k4 = lax.broadcasted_iota(I32, (tm, TOP_K), 1)
    vals = jnp.zeros((tm, TOP_K), F32)
    idxs = jnp.zeros((tm, TOP_K), F32)
    for k in range(TOP_K):
        mx = jnp.max(logits, axis=1, keepdims=True)
        ix = jnp.min(jnp.where(logits == mx, lane, float(N_EXPERTS)), axis=1, keepdims=True)
        vals = jnp.where(k4 == k, mx, vals)
        idxs = jnp.where(k4 == k, ix, idxs)
        logits = jnp.where(lane == ix, -jnp.inf, logits)
    e = jnp.exp(vals - jnp.max(vals, axis=1, keepdims=True))
    idx_ref[...] = idxs.astype(I32)
    gate_ref[...] = e * (1.0 / jnp.sum(e, axis=1, keepdims=True))


def _merge(h, hb, oa, ob, oc, wg, bg, wbr, wout, g, b, wr, br):
    n, d = h.shape
    tm = _tile(n, 512)
    row = lambda w: pl.BlockSpec((tm, w), lambda i: (i, 0))
    return pl.pallas_call(
        _merge_kernel, grid=(n // tm,),
        in_specs=[row(d), row(d), row(BRANCH_W), row(BRANCH_W), row(BRANCH_W),
                  _full(wg.shape), _full(bg.shape), _full(wbr.shape), _full(wout.shape),
                  _full(g.shape), _full(b.shape), _full(wr.shape), _full(br.shape)],
        out_specs=[row(d), row(d), row(TOP_K), row(TOP_K)],
        out_shape=[jax.ShapeDtypeStruct((n, d), F32), jax.ShapeDtypeStruct((n, d), BF16),
                   jax.ShapeDtypeStruct((n, TOP_K), I32), jax.ShapeDtypeStruct((n, TOP_K), F32)],
        compiler_params=_cparams("parallel"),
    )(h, hb, oa, ob, oc, wg, bg, wbr, wout, g, b, wr, br)


def _expert_kernel(blk_e_ref, idx_hbm, x_hbm, w1_ref, b1_ref, w2_ref, b2_ref, y_hbm,
                   idx_smem, xbuf, ybuf, isem, gsem, ssem, *, bs, n_blocks):
    del blk_e_ref
    i = pl.program_id(0)

    def idx_copy(row, s):
        return pltpu.make_async_copy(idx_hbm.at[row], idx_smem.at[s], isem.at[s])

    def start_gather(s):
        for r in range(bs):
            tok = idx_smem[s, r]
            pltpu.make_async_copy(x_hbm.at[pl.ds(tok, 1)], xbuf.at[s, pl.ds(r, 1)], gsem.at[s]).start()

    def wait_gather(s):
        pltpu.make_async_copy(x_hbm.at[pl.ds(0, bs)], xbuf.at[s], gsem.at[s]).wait()

    def start_scatter(s_buf, s_idx):
        for r in range(bs):
            dst = idx_smem[s_idx, bs + r]
            pltpu.make_async_copy(ybuf.at[s_buf, pl.ds(r, 1)], y_hbm.at[pl.ds(dst, 1)], ssem.at[s_buf]).start()

    def wait_scatter(s):
        pltpu.make_async_copy(ybuf.at[s], y_hbm.at[pl.ds(0, bs)], ssem.at[s]).wait()

    def step(slot, gather_next, scatter_prev):
        nslot = 1 - slot
        idx_copy(i + 2, slot).start()
        idx_copy(i + 1, nslot).wait()
        wait_gather(slot)

        @pl.when(i >= 2)
        def _():
            wait_scatter(slot)

        if gather_next:
            start_gather(nslot)
        if scatter_prev:
            start_scatter(nslot, nslot)
        x = xbuf[slot].astype(BF16)
        gu = _dot(x, w1_ref[0]) + b1_ref[0]
        g = jnp.minimum(gu[:, :D_FF], SWIGLU_LIMIT)
        u = jnp.clip(gu[:, D_FF:], -SWIGLU_LIMIT, SWIGLU_LIMIT)
        a = g * (1.0 / (1.0 + jnp.exp(-SWIGLU_ALPHA * g))) * (u + 1.0)
        ybuf[slot] = _dot(a.astype(BF16), w2_ref[0]) + b2_ref[0]

    last = n_blocks - 1
    middle = jnp.logical_and(i > 0, i < last)

    @pl.when(i == 0)
    def _first():
        idx_copy(0, 0).start()
        idx_copy(0, 0).wait()
        start_gather(0)
        idx_copy(1, 1).start()
        step(0, True, False)

    @pl.when(jnp.logical_and(middle, i % 2 == 0))
    def _middle_even():
        step(0, True, True)

    @pl.when(jnp.logical_and(middle, i % 2 == 1))
    def _middle_odd():
        step(1, True, True)

    @pl.when(i == last)
    def _last():
        slot = last % 2
        step(slot, False, True)
        idx_copy(i + 2, slot).wait()
        start_scatter(slot, slot)
        wait_scatter(1 - slot)
        wait_scatter(slot)


def _experts(x, blk_e, idx, w1, b1, w2, b2, n_rows_out, bs):
    n, d = x.shape
    n_blocks = idx.shape[0] - 2
    assert n_blocks >= 2
    kern = functools.partial(_expert_kernel, bs=bs, n_blocks=n_blocks)
    grid_spec = pltpu.PrefetchScalarGridSpec(
        num_scalar_prefetch=1, grid=(n_blocks,),
        in_specs=[pl.BlockSpec(memory_space=pl.ANY), pl.BlockSpec(memory_space=pl.ANY),
                  pl.BlockSpec((1, d, 2 * D_FF), lambda i, be: (be[i], 0, 0)),
                  pl.BlockSpec((1, 1, 2 * D_FF), lambda i, be: (be[i], 0, 0)),
                  pl.BlockSpec((1, D_FF, d), lambda i, be: (be[i], 0, 0)),
                  pl.BlockSpec((1, 1, d), lambda i, be: (be[i], 0, 0))],
        out_specs=pl.BlockSpec(memory_space=pl.ANY),
        scratch_shapes=[pltpu.SMEM((2, 2 * bs), I32),
                        pltpu.VMEM((2, bs, d), F32), pltpu.VMEM((2, bs, d), F32),
                        pltpu.SemaphoreType.DMA((2,)), pltpu.SemaphoreType.DMA((2,)),
                        pltpu.SemaphoreType.DMA((2,))])
    return pl.pallas_call(
        kern, grid_spec=grid_spec,
        out_shape=jax.ShapeDtypeStruct((n_rows_out, d), F32),
        compiler_params=_cparams("arbitrary"),
    )(blk_e, idx, x, w1, b1, w2, b2)


def _dispatch_tables(top_idx, bs):
    n = top_idx.shape[0]
    m = n * TOP_K
    flat_e = top_idx.reshape(-1)
    order = jnp.argsort(flat_e).astype(I32)
    counts = jnp.bincount(flat_e, length=N_EXPERTS).astype(I32)
    padded = (counts + bs - 1) // bs * bs
    start = jnp.cumsum(counts) - counts
    pend = jnp.cumsum(padded)
    pstart = pend - padded
    n_blocks = -(-m // bs) + N_EXPERTS
    cap = n_blocks * bs
    blk_start = jnp.arange(n_blocks, dtype=I32) * bs
    blk_e = jnp.minimum(jnp.sum((pend[None, :] <= blk_start[:, None]).astype(I32), axis=1), N_EXPERTS - 1)
    slot = jnp.arange(cap, dtype=I32)
    per_slot = lambda per_blk: jnp.broadcast_to(per_blk[:, None], (n_blocks, bs)).reshape(cap)
    e = per_slot(blk_e)
    is_pad = slot - per_slot(pstart[blk_e]) >= per_slot(counts[blk_e])
    ext = jnp.concatenate([jnp.zeros((cap,), I32), order, jnp.zeros((cap,), I32)])
    shift = pstart - start

    def place(ex, acc):
        return jnp.where(e == ex, lax.dynamic_slice(ext, (cap - shift[ex],), (cap,)), acc)

    flat = lax.fori_loop(0, N_EXPERTS, place, jnp.zeros((cap,), I32))
    src = jnp.where(is_pad, 0, flat // TOP_K)
    pads_before = slot - per_slot((start + counts)[blk_e])
    dst = jnp.where(is_pad, m + pads_before, (flat % TOP_K) * n + flat // TOP_K)
    zeros = jnp.zeros((2, bs), I32)
    idx = jnp.concatenate([jnp.concatenate([src.reshape(n_blocks, bs), zeros], axis=0),
                           jnp.concatenate([zeros, dst.reshape(n_blocks, bs)], axis=0)], axis=1)
    return blk_e, idx, cap


def _combine_kernel(x_ref, y0_ref, y1_ref, y2_ref, y3_ref, gate_ref, g_ref, b_ref, o_ref, ob_ref):
    y = gate_ref[:, 0:1] * y0_ref[...]
    for k, y_ref in enumerate((y1_ref, y2_ref, y3_ref), start=1):
        y = y + gate_ref[:, k:k + 1] * y_ref[...]
    out = _layer_norm(DN_ALPHA * x_ref[...] + y, g_ref[...], b_ref[...])
    o_ref[...] = out
    ob_ref[...] = out.astype(BF16)


def _combine(x, y, gate, g, b):
    n, d = x.shape
    tm = _tile(n, 256)
    nt = n // tm
    row = lambda w: pl.BlockSpec((tm, w), lambda i: (i, 0))
    y_specs = [pl.BlockSpec((tm, d), lambda i, k=k: (k * nt + i, 0)) for k in range(TOP_K)]
    return pl.pallas_call(
        _combine_kernel, grid=(nt,),
        in_specs=[row(d)] + y_specs + [row(TOP_K), _full((1, d)), _full((1, d))],
        out_specs=[row(d), row(d)],
        out_shape=[jax.ShapeDtypeStruct((n, d), F32), jax.ShapeDtypeStruct((n, d), BF16)],
        compiler_params=_cparams("parallel"),
    )(x, y, y, y, y, gate, g, b)


def _deinterleave_kernel(w_ref, sel_ref, o_ref):
    half = o_ref.shape[2] // 2
    for s in range(half // LANES):
        y = _dot(w_ref[0, :, 2 * s * LANES:2 * (s + 1) * LANES].astype(BF16), sel_ref[...])
        o_ref[0, :, s * LANES:(s + 1) * LANES] = y[:, :LANES].astype(BF16)
        o_ref[0, :, half + s * LANES:half + (s + 1) * LANES] = y[:, LANES:].astype(BF16)


def _cast_kernel(w_ref, o_ref):
    o_ref[...] = w_ref[...].astype(BF16)


def _cast_down(w_down):
    L, E, f, d = w_down.shape
    blk = pl.BlockSpec((1, f, d), lambda i: (i, 0, 0))
    out = pl.pallas_call(
        _cast_kernel, grid=(L * E,), in_specs=[blk], out_specs=blk,
        out_shape=jax.ShapeDtypeStruct((L * E, f, d), BF16),
        compiler_params=_cparams("parallel"),
    )(w_down.reshape(L * E, f, d))
    return out.reshape(L, E, f, d)


def _deinterleave_up(w_up):
    L, E, d, f2 = w_up.shape
    j = jnp.arange(2 * LANES)
    sel = (j[:, None] == jnp.where(j < LANES, 2 * j, 2 * (j - LANES) + 1)[None, :]).astype(BF16)
    out = pl.pallas_call(
        _deinterleave_kernel, grid=(L * E,),
        in_specs=[pl.BlockSpec((1, d, f2), lambda i: (i, 0, 0)), _full((2 * LANES, 2 * LANES))],
        out_specs=pl.BlockSpec((1, d, f2), lambda i: (i, 0, 0)),
        out_shape=jax.ShapeDtypeStruct((L * E, d, f2), BF16),
        compiler_params=_cparams("parallel"),
    )(w_up.reshape(L * E, d, f2), sel)
    return out.reshape(L, E, d, f2)


def _prep_layer(l, w_in, b_f, b_gate, g_qa, g_kva, w_uq, w_ukv, w_br, w_out, ln1_g, ln1_b,
                w_router, b_router, w_up, b_up, w_down, b_down, ln2_g, ln2_b):
    w = w_in[l]
    o = [0]

    def take(width):
        s = w[:, o[0]:o[0] + width]
        o[0] += width
        return s

    def pad_to(a, width):
        return jnp.pad(a, ((0, 0), (0, width - a.shape[1])))

    qa, kva, kra = take(Q_LORA), take(KV_LORA), take(ROPE_A)
    qb, kb, vb, fb = take(BRANCH_W), take(BRANCH_W), take(BRANCH_W), take(N_HEADS)
    qc, kc, vc = take(BRANCH_W), take(BRANCH_W), take(BRANCH_W)
    iqc, ikc, iwc = take(IDX_H * IDX_DIM), take(IDX_DIM), take(IDX_H)
    gt = take(N_BRANCH * D_MODEL)
    p = {}
    p['w_mla'] = jnp.concatenate([qa, kva, jnp.tile(kra, (1, 4))], axis=1).astype(BF16)
    p['w_fox'] = jnp.concatenate([qb, kb, vb, pad_to(fb, LANES)], axis=1).astype(BF16)
    p['w_dsa'] = jnp.concatenate([qc, kc, vc, iqc, jnp.tile(ikc, (1, 4)), pad_to(iwc, LANES)], axis=1).astype(BF16)
    p['w_gate'] = gt.astype(BF16)
    uq = w_uq[l].reshape(Q_LORA, N_HEADS, HEAD_DIM + ROPE_A)
    p['w_uq'] = jnp.concatenate([uq[:, :, :HEAD_DIM].reshape(Q_LORA, -1),
                                 uq[:, :, HEAD_DIM:].reshape(Q_LORA, -1)], axis=1).astype(BF16)
    ukv = w_ukv[l].reshape(KV_LORA, N_HEADS, 2 * HEAD_DIM)
    p['w_ukv'] = jnp.concatenate([ukv[:, :, :HEAD_DIM].reshape(KV_LORA, -1),
                                  ukv[:, :, HEAD_DIM:].reshape(KV_LORA, -1)], axis=1).astype(BF16)
    p['b_f'] = b_f[l].reshape(1, N_HEADS)
    p['b_gate'] = b_gate[l].reshape(1, N_BRANCH * D_MODEL)
    p['g_qa'] = g_qa[l].reshape(1, Q_LORA)
    p['g_kva'] = g_kva[l].reshape(1, KV_LORA)
    p['w_br'] = w_br[l].astype(BF16)
    p['w_out'] = w_out[l].astype(BF16)
    p['ln1'] = (ln1_g[l].reshape(1, -1), ln1_b[l].reshape(1, -1))
    p['ln2'] = (ln2_g[l].reshape(1, -1), ln2_b[l].reshape(1, -1))
    p['w_router'] = w_router[l].astype(BF16)
    p['b_router'] = b_router[l].reshape(1, N_EXPERTS)
    p['w_up'] = w_up[l]
    p['b_up'] = jnp.concatenate([b_up[l][:, 0::2], b_up[l][:, 1::2]], axis=1)[:, None, :]
    p['w_down'] = w_down[l]
    p['b_down'] = b_down[l][:, None, :]
    return p


def _pad_time(a, t_pad):
    return a if a.shape[1] == t_pad else jnp.pad(a, ((0, 0), (0, t_pad - a.shape[1])) + ((0, 0),) * (a.ndim - 2))


def _trunk(x, cache, params, ln_in, moe_block):
    B, T, D = x.shape
    n = B * T
    past = 0 if cache is None else cache[0].shape[2]
    n_keys = past + T
    tkp = -(-n_keys // LANES) * LANES
    pos = jnp.arange(past, past + T, dtype=I32)
    pb, pt = (B, T) if T % LANES == 0 else (1, n)
    tables = lambda rot, period: tuple(jnp.tile(t, (n // (pb * T), 1)) for t in _rope_tables(pos, rot, period))
    tab_a = tables(ROPE_A, ROPE_A)
    tab_c = tables(ROT_C, HEAD_DIM)
    tab_i = tables(ROT_IDX, IDX_DIM)
    tile_mat = (jnp.arange(ROPE_A)[:, None] == (jnp.arange(LANES)[None, :] % ROPE_A)).astype(BF16)
    tm = _tile(pt, 512)
    key_major = T % LANES == 0 and tkp == n_keys

    def project(kernel, hb, weights, tabs, outs):
        return [o.reshape(B, T, -1) for o in _proj_call(kernel, hb.reshape(pb, pt, D), weights, tabs, outs, tm)]

    h, hb = _ln_in(x.reshape(n, D), *ln_in)
    rows = [[] for _ in range(8)]
    for l, p in enumerate(params):
        qn, qr, ckv_new, kr_new = project(
            _mla_proj_kernel, hb, [p['w_mla'], p['w_uq'], p['g_qa'], p['g_kva']], tab_a,
            [(BRANCH_W, BF16), (2 * LANES, BF16), (KV_LORA, F32), (ROPE_A, F32)])
        qb, kb_new, vb_new, kb16, vb16, lf_new = project(
            _fox_proj_kernel, hb, [p['w_fox'], p['b_f']], (),
            [(BRANCH_W, BF16), (BRANCH_W, F32), (BRANCH_W, F32), (BRANCH_W, BF16), (BRANCH_W, BF16),
             (N_HEADS, F32)])
        qc, kc_new, vc_new, kc16, vc16, iq, ik_new, iw = project(
            _dsa_proj_kernel, hb, [p['w_dsa']], tab_c + tab_i,
            [(BRANCH_W, BF16), (BRANCH_W, F32), (BRANCH_W, F32), (BRANCH_W, BF16), (BRANCH_W, BF16),
             (IDX_H * IDX_DIM, BF16), (IDX_DIM, F32), (IDX_H, F32)])

        if cache is None:
            ckv_all, kr_all, lf_all, ik_all = ckv_new, kr_new, lf_new, ik_new
            kb_all, vb_all, kc_all, vc_all = kb16, vb16, kc16, vc16
        else:
            c_ckv, c_kr, c_kb, c_vb, c_lf, c_kc, c_vc, c_ik = (c[l] for c in cache)
            flat = lambda c: c.reshape(B, past, BRANCH_W).astype(BF16)
            ckv_all = jnp.concatenate([c_ckv, ckv_new], axis=1)
            kr_all = jnp.concatenate([c_kr, kr_new], axis=1)
            lf_all = jnp.concatenate([c_lf, lf_new], axis=1)
            ik_all = jnp.concatenate([c_ik, ik_new], axis=1)
            kb_all = jnp.concatenate([flat(c_kb), kb16], axis=1)
            vb_all = jnp.concatenate([flat(c_vb), vb16], axis=1)
            kc_all = jnp.concatenate([flat(c_kc), kc16], axis=1)
            vc_all = jnp.concatenate([flat(c_vc), vc16], axis=1)
        ckv_all, kr_all, lf_all, ik_all, kb_all, vb_all, kc_all, vc_all = (
            _pad_time(a, tkp) for a in (ckv_all, kr_all, lf_all, ik_all, kb_all, vb_all, kc_all, vc_all))

        kpp, v_a = _mla_kv(ckv_all, kr_all, p['w_ukv'], tile_mat)
        cum = _cumsum_time(jnp.swapaxes(lf_all, 1, 2))
        if key_major:
            o_a = _mla_attention_t(qn, qr, kpp, v_a, past, n_keys)
            o_b = _fox_attention_t(qb, kb_all, vb_all, cum[:, :, past:past + T], jnp.swapaxes(cum, 1, 2), past)
            ik4 = jnp.tile(ik_all.astype(BF16), (1, 1, LANES // IDX_DIM))
            o_c = _dsa_attention_t(qc, kc_all, vc_all, iq, ik4, jnp.swapaxes(iw, 1, 2), past, n_keys)
        else:
            o_a = _mla_attention(qn, qr, kpp, v_a, past, n_keys)
            cq = jnp.swapaxes(cum[:, :, past:past + T], 1, 2)
            o_b = _fox_attention(qb, kb_all, vb_all, cq, cum, past)
            tk = _tile(tkp, 512)
            ikt = jnp.swapaxes(ik_all.astype(BF16), 1, 2)
            ikt = jnp.tile(ikt, (1, LANES // IDX_DIM, 1))
            ikt = jnp.swapaxes(ikt.reshape(B, LANES, tkp // tk, tk), 1, 2)
            o_c = _dsa_attention(qc, kc_all, vc_all, iq, ikt, iw, past, n_keys)

        x1, x1b, top_idx, gate = _merge(
            h, hb, o_a.reshape(n, -1), o_b.reshape(n, -1), o_c.reshape(n, -1),
            p['w_gate'], p['b_gate'], p['w_br'], p['w_out'], *p['ln1'], p['w_router'], p['b_router'])

        blk_e, idx, n_rows = _dispatch_tables(top_idx, moe_block)
        y4 = _experts(x1, blk_e, idx, p['w_up'], p['b_up'], p['w_down'], p['b_down'], n_rows, moe_block)
        h, hb = _combine(x1, y4, gate, *p['ln2'])

        new = (ckv_new, kr_new, kb_new.reshape(B, T, N_HEADS, HEAD_DIM), vb_new.reshape(B, T, N_HEADS, HEAD_DIM),
               lf_new, kc_new.reshape(B, T, N_HEADS, HEAD_DIM), vc_new.reshape(B, T, N_HEADS, HEAD_DIM), ik_new)
        for acc, r in zip(rows, new):
            acc.append(r)
    return h.reshape(B, T, D), [jnp.stack(a) for a in rows]


def kernel(x_prompt, x_sample, cache_mla_ckv, cache_mla_krope, cache_fox_k, cache_fox_v, cache_fox_logf,
           cache_dsa_k, cache_dsa_v, cache_dsa_idxk, ln_in_g, ln_in_b, w_in, b_f, b_gate, g_qa, g_kva,
           w_uq, w_ukv, w_br, w_out, ln1_g, ln1_b, w_router, b_router, w_up, b_up, w_down, b_down,
           ln2_g, ln2_b):
    depth = w_in.shape[0]
    w_up = _deinterleave_up(w_up)
    w_down = _cast_down(w_down)
    params = [_prep_layer(l, w_in, b_f, b_gate, g_qa, g_kva, w_uq, w_ukv, w_br, w_out, ln1_g, ln1_b,
                          w_router, b_router, w_up, b_up, w_down, b_down, ln2_g, ln2_b) for l in range(depth)]
    ln_in = (ln_in_g, ln_in_b)
    y_p, st_p = _trunk(x_prompt, None, params, ln_in, moe_block=512)
    caches = (cache_mla_ckv, cache_mla_krope, cache_fox_k, cache_fox_v, cache_fox_logf,
              cache_dsa_k, cache_dsa_v, cache_dsa_idxk)
    y_s, st_s = _trunk(x_sample, caches, params, ln_in, moe_block=128)
    out = [y_p, y_s]
    for a, b in zip(st_p, st_s):
        out += [a, b]
    return tuple(out)
```

```python
import functools

import jax
import jax.numpy as jnp
from jax import lax
from jax.experimental import pallas as pl
from jax.experimental.pallas import tpu as pltpu

F32 = jnp.float32
BF16 = jnp.bfloat16
I32 = jnp.int32

D_MODEL = 1024
CHUNK = 64
CHUNK_SHIFT = 6
ROPE_THETA = 500000.0
LN_EPS = 1e-5
RMS_EPS = 1e-6
DEPTH_NOMINAL = 4
DN_ALPHA = (2 * DEPTH_NOMINAL) ** 0.25
N_HEADS = 8
HEAD_DIM = 64
Q_LORA = 384
KV_LORA = 256
ROPE_A = 32
ROT_C = 16
IDX_H = 8
IDX_DIM = 32
ROT_IDX = 8
TOPK_MAX = 256
N_BRANCH = 3
BRANCH_W = N_HEADS * HEAD_DIM
N_EXPERTS = 32
TOP_K = 4
D_FF = 1024
SWIGLU_LIMIT = 7.0
SWIGLU_ALPHA = 1.702

LANES = 128
ONES_ROWS = 16
VMEM_LIMIT_BYTES = 56 * 2**20

LOG2_E = 1.4426950408889634
MLA_Q_SCALE = (HEAD_DIM + ROPE_A) ** -0.5 * LOG2_E
NEG = -1e30
INT_MIN = -2**31
KEY_NEG_INF = -2139095041
BIG_IDX = 2**30


def _cparams(*sem):
    return pltpu.CompilerParams(dimension_semantics=sem, vmem_limit_bytes=VMEM_LIMIT_BYTES)


def _tile(n, pref):
    return pref if n % pref == 0 else n


def _full(shape):
    zeros = (0,) * len(shape)
    return pl.BlockSpec(shape, lambda *_: zeros)


def _dot(a, b):
    return jnp.dot(a, b, preferred_element_type=F32)


def _dot_nt(a, b):
    return lax.dot_general(a, b, (((1,), (1,)), ((), ())), preferred_element_type=F32)


def _layer_norm(x, g, b):
    xc = x - jnp.mean(x, axis=-1, keepdims=True)
    var = jnp.mean(xc * xc, axis=-1, keepdims=True)
    return xc * lax.rsqrt(var + LN_EPS) * g + b


def _rms_norm(x, g):
    return x * lax.rsqrt(jnp.mean(x * x, axis=-1, keepdims=True) + RMS_EPS) * g


def _rope_slab(x, c, sa, sb, half):
    return x * c + pltpu.roll(x, half, 1) * sa + pltpu.roll(x, LANES - half, 1) * sb


def _rope_tables(pos, rot, period):
    half = rot // 2
    inv_freq = ROPE_THETA ** (-jnp.arange(half, dtype=F32) / half)
    ang = pos.astype(F32)[:, None] * inv_freq
    j = jnp.arange(LANES) % period
    cos = jnp.cos(ang)[:, j % half]
    sin = jnp.sin(ang)[:, j % half]
    c = jnp.where(j < rot, cos, 1.0)
    sa = jnp.where((j >= half) & (j < rot), sin, 0.0)
    sb = jnp.where(j < half, -sin, 0.0)
    return c, sa, sb


def _ln_kernel(x_ref, g_ref, b_ref, o_ref, ob_ref):
    y = _layer_norm(x_ref[...], g_ref[...], b_ref[...])
    o_ref[...] = y
    ob_ref[...] = y.astype(BF16)


def _ln_in(x, g, b):
    n, d = x.shape
    tm = _tile(n, 512)
    row = pl.BlockSpec((tm, d), lambda i: (i, 0))
    return pl.pallas_call(
        _ln_kernel, grid=(n // tm,),
        in_specs=[row, _full((1, d)), _full((1, d))],
        out_specs=[row, row],
        out_shape=[jax.ShapeDtypeStruct((n, d), F32), jax.ShapeDtypeStruct((n, d), BF16)],
        compiler_params=_cparams("parallel"),
    )(x, g.reshape(1, d), b.reshape(1, d))


def _mla_proj_kernel(hb_ref, w_ref, wuq_ref, gqa_ref, gkva_ref, c_ref, sa_ref, sb_ref,
                     qn_ref, qr_ref, ckv_ref, kr_ref):
    z = _dot(hb_ref[0], w_ref[...])
    qa = _rms_norm(z[:, :Q_LORA], gqa_ref[...])
    q = _dot(qa.astype(BF16), wuq_ref[...]) * MLA_Q_SCALE
    qn_ref[0] = q[:, :BRANCH_W].astype(BF16)
    c, sa, sb = c_ref[...], sa_ref[...], sb_ref[...]
    for s in range(2):
        lo = BRANCH_W + s * LANES
        qr_ref[0, :, s * LANES:(s + 1) * LANES] = _rope_slab(q[:, lo:lo + LANES], c, sa, sb, ROPE_A // 2).astype(BF16)
    ckv_ref[0] = _rms_norm(z[:, Q_LORA:Q_LORA + KV_LORA], gkva_ref[...])
    kr = _rope_slab(z[:, Q_LORA + KV_LORA:], c, sa, sb, ROPE_A // 2)
    kr_ref[0] = kr[:, :ROPE_A]


def _fox_proj_kernel(hb_ref, w_ref, bf_ref, q_ref, k_ref, v_ref, k16_ref, v16_ref, lf_ref):
    z = _dot(hb_ref[0], w_ref[...])
    w = BRANCH_W
    q_ref[0] = (z[:, :w] * HEAD_DIM ** -0.5).astype(BF16)
    k = z[:, w:2 * w]
    v = z[:, 2 * w:3 * w]
    k_ref[0] = k
    v_ref[0] = v
    k16_ref[0] = k.astype(BF16)
    v16_ref[0] = v.astype(BF16)
    x = z[:, 3 * w:3 * w + N_HEADS] + bf_ref[...]
    lf_ref[0] = jnp.minimum(x, 0.0) - jnp.log1p(jnp.exp(-jnp.abs(x)))


def _dsa_proj_kernel(hb_ref, w_ref, c64_ref, sa64_ref, sb64_ref, c32_ref, sa32_ref, sb32_ref,
                     q_ref, k_ref, v_ref, k16_ref, v16_ref, iq_ref, ik_ref, iw_ref):
    z = _dot(hb_ref[0], w_ref[...])
    w = BRANCH_W
    c64, sa64, sb64 = c64_ref[...], sa64_ref[...], sb64_ref[...]
    c32, sa32, sb32 = c32_ref[...], sa32_ref[...], sb32_ref[...]
    for s in range(w // LANES):
        sl = slice(s * LANES, (s + 1) * LANES)
        qs = _rope_slab(z[:, s * LANES:(s + 1) * LANES], c64, sa64, sb64, ROT_C // 2)
        q_ref[0, :, sl] = (qs * HEAD_DIM ** -0.5).astype(BF16)
        ks = _rope_slab(z[:, w + s * LANES:w + (s + 1) * LANES], c64, sa64, sb64, ROT_C // 2)
        k_ref[0, :, sl] = ks
        k16_ref[0, :, sl] = ks.astype(BF16)
    v = z[:, 2 * w:3 * w]
    v_ref[0] = v
    v16_ref[0] = v.astype(BF16)
    o = 3 * w
    for s in range(2):
        iq_ref[0, :, s * LANES:(s + 1) * LANES] = _rope_slab(
            z[:, o + s * LANES:o + (s + 1) * LANES], c32, sa32, sb32, ROT_IDX // 2).astype(BF16)
    o += IDX_H * IDX_DIM
    ik = _rope_slab(z[:, o:o + LANES], c32, sa32, sb32, ROT_IDX // 2)
    ik_ref[0] = ik[:, :IDX_DIM]
    o += LANES
    iw_ref[0] = z[:, o:o + IDX_H] * (IDX_DIM ** -0.5 * IDX_H ** -0.5)


def _proj_call(kernel, hb, weights, tables, out_widths_dtypes, tm):
    B, T, D = hb.shape
    grid = (B, T // tm)
    in_specs = [pl.BlockSpec((1, tm, D), lambda b, t: (b, t, 0))]
    in_specs += [_full(w.shape) for w in weights]
    in_specs += [pl.BlockSpec((tm, LANES), lambda b, t: (t, 0)) for _ in tables]
    out_specs = [pl.BlockSpec((1, tm, w), lambda b, t: (b, t, 0)) for w, _ in out_widths_dtypes]
    out_shape = [jax.ShapeDtypeStruct((B, T, w), dt) for w, dt in out_widths_dtypes]
    return pl.pallas_call(kernel, grid=grid, in_specs=in_specs, out_specs=out_specs, out_shape=out_shape,
                          compiler_params=_cparams("parallel", "parallel"))(hb, *weights, *tables)


def _mla_kv_kernel(ckv_ref, kr_ref, w_ref, tile_ref, kpp_ref, v_ref):
    kv = _dot(ckv_ref[0].astype(BF16), w_ref[...])
    kr4 = _dot(kr_ref[0].astype(BF16), tile_ref[...]).astype(BF16)
    for p in range(N_HEADS // 2):
        kpp_ref[0, p, :, :LANES] = kv[:, p * LANES:(p + 1) * LANES].astype(BF16)
        kpp_ref[0, p, :, LANES:] = kr4
    v_ref[0] = kv[:, BRANCH_W:].astype(BF16)


def _mla_kv(ckv, kr, w_ukv_p, tile_mat):
    B, Tk, _ = ckv.shape
    tm = _tile(Tk, 512)
    return pl.pallas_call(
        _mla_kv_kernel, grid=(B, Tk // tm),
        in_specs=[pl.BlockSpec((1, tm, KV_LORA), lambda b, t: (b, t, 0)),
                  pl.BlockSpec((1, tm, ROPE_A), lambda b, t: (b, t, 0)),
                  _full(w_ukv_p.shape), _full(tile_mat.shape)],
        out_specs=[pl.BlockSpec((1, N_HEADS // 2, tm, 2 * LANES), lambda b, t: (b, 0, t, 0)),
                   pl.BlockSpec((1, tm, BRANCH_W), lambda b, t: (b, t, 0))],
        out_shape=[jax.ShapeDtypeStruct((B, N_HEADS // 2, Tk, 2 * LANES), BF16),
                   jax.ShapeDtypeStruct((B, Tk, BRANCH_W), BF16)],
        compiler_params=_cparams("parallel", "parallel"),
    )(ckv, kr, w_ukv_p, tile_mat)


def _cumsum_kernel(x_ref, u_ref, o_ref, carry_ref):
    @pl.when(pl.program_id(1) == 0)
    def _():
        carry_ref[...] = jnp.zeros_like(carry_ref)

    x = x_ref[0]
    u = u_ref[...]
    hi = x.astype(BF16)
    r1 = x - hi.astype(F32)
    mid = r1.astype(BF16)
    lo = (r1 - mid.astype(F32)).astype(BF16)
    out = _dot(hi, u) + _dot(mid, u) + _dot(lo, u) + carry_ref[...]
    o_ref[0] = out
    tc = x.shape[1]
    carry_ref[...] = out[:, tc - 1:tc]


def _cumsum_time(x):
    B, H, Tk = x.shape
    tc = _tile(Tk, 512)
    u = (jnp.arange(tc)[:, None] <= jnp.arange(tc)[None, :]).astype(BF16)
    return pl.pallas_call(
        _cumsum_kernel, grid=(B, Tk // tc),
        in_specs=[pl.BlockSpec((1, H, tc), lambda b, t: (b, 0, t)), _full((tc, tc))],
        out_specs=pl.BlockSpec((1, H, tc), lambda b, t: (b, 0, t)),
        out_shape=jax.ShapeDtypeStruct((B, H, Tk), F32),
        scratch_shapes=[pltpu.VMEM((H, 1), F32)],
        compiler_params=_cparams("parallel", "arbitrary"),
    )(x, u)


def _own_lanes(lane, h, width):
    per = LANES // width
    return jnp.right_shift(lane, width.bit_length() - 1) == (h % per)


def _keep_head(x, lane, h, width):
    return jnp.where(_own_lanes(lane, h, width), x.astype(F32), 0.0).astype(BF16)


def _softmax_step(h, s, v_slab, m_ref, l_ref, acc_ref, exp=jnp.exp):
    m_prev = m_ref[h]
    m_new = jnp.maximum(m_prev, jnp.max(s, axis=1, keepdims=True))
    alpha = exp(m_prev - m_new)
    p = exp(s - m_new)
    l_ref[h] = alpha * l_ref[h] + jnp.sum(p, axis=1, keepdims=True)
    acc_ref[h] = alpha * acc_ref[h] + _dot(p.astype(BF16), v_slab)
    m_ref[h] = m_new


def _init_softmax(m_ref, l_ref, acc_ref):
    m_ref[...] = jnp.full(m_ref.shape, NEG, F32)
    l_ref[...] = jnp.zeros(l_ref.shape, F32)
    acc_ref[...] = jnp.zeros(acc_ref.shape, F32)


def _write_heads(o_ref, l_ref, acc_ref, lane):
    for p in range(N_HEADS // 2):
        a = acc_ref[2 * p] * (1.0 / l_ref[2 * p])
        b = acc_ref[2 * p + 1] * (1.0 / l_ref[2 * p + 1])
        o_ref[0, :, p * LANES:(p + 1) * LANES] = jnp.where(lane < HEAD_DIM, a, b).astype(BF16)


def _softmax_scratch(tq):
    return [pltpu.VMEM((N_HEADS, tq, 1), F32), pltpu.VMEM((N_HEADS, tq, 1), F32),
            pltpu.VMEM((N_HEADS, tq, LANES), F32)]


def _mla_attn_kernel(qn_ref, qr_ref, kpp_ref, v_ref, o_ref, qs_ref, m_ref, l_ref, acc_ref,
                     *, tq, tk, past, n_keys, padded):
    qi = pl.program_id(1)
    ki = pl.program_id(2)
    q0 = past + qi * tq
    last_k = (jnp.minimum(n_keys, ((q0 + tq - 1) // CHUNK + 1) * CHUNK) - 1) // tk
    lane = lax.broadcasted_iota(I32, (tq, LANES), 1)

    @pl.when(ki == 0)
    def _init():
        _init_softmax(m_ref, l_ref, acc_ref)
        for h in range(N_HEADS):
            qn = qn_ref[0, :, (h // 2) * LANES:(h // 2 + 1) * LANES]
            qr = qr_ref[0, :, (h // 4) * LANES:(h // 4 + 1) * LANES]
            qs_ref[h, :, :LANES] = _keep_head(qn, lane, h, HEAD_DIM)
            qs_ref[h, :, LANES:] = _keep_head(qr, lane, h, ROPE_A)

    @pl.when(ki <= last_k)
    def _step():
        qpos = q0 + lax.broadcasted_iota(I32, (tq, tk), 0)
        kpos = ki * tk + lax.broadcasted_iota(I32, (tq, tk), 1)
        kchunk = jnp.right_shift(kpos, CHUNK_SHIFT)
        if padded:
            kchunk = jnp.where(kpos < n_keys, kchunk, BIG_IDX)
        ok = kchunk <= jnp.right_shift(qpos, CHUNK_SHIFT)
        for h in range(N_HEADS):
            p = h // 2
            s = jnp.where(ok, _dot_nt(qs_ref[h], kpp_ref[0, p]), NEG)
            _softmax_step(h, s, v_ref[0, :, p * LANES:(p + 1) * LANES], m_ref, l_ref, acc_ref, exp=jnp.exp2)

    @pl.when(ki == pl.num_programs(2) - 1)
    def _fin():
        _write_heads(o_ref, l_ref, acc_ref, lane)


def _mla_attention(qn, qr, kpp, v, past, n_keys):
    B, T, _ = qn.shape
    Tkp = v.shape[1]
    tq = _tile(T, 512)
    tk = _tile(Tkp, 512)

    def kblk(qi, ki):
        last = (jnp.minimum(n_keys, ((past + qi * tq + tq - 1) // CHUNK + 1) * CHUNK) - 1) // tk
        return jnp.minimum(ki, last)

    kern = functools.partial(_mla_attn_kernel, tq=tq, tk=tk, past=past, n_keys=n_keys, padded=Tkp != n_keys)
    return pl.pallas_call(
        kern, grid=(B, T // tq, Tkp // tk),
        in_specs=[pl.BlockSpec((1, tq, BRANCH_W), lambda b, qi, ki: (b, qi, 0)),
                  pl.BlockSpec((1, tq, 2 * LANES), lambda b, qi, ki: (b, qi, 0)),
                  pl.BlockSpec((1, N_HEADS // 2, tk, 2 * LANES), lambda b, qi, ki: (b, 0, kblk(qi, ki), 0)),
                  pl.BlockSpec((1, tk, BRANCH_W), lambda b, qi, ki: (b, kblk(qi, ki), 0))],
        out_specs=pl.BlockSpec((1, tq, BRANCH_W), lambda b, qi, ki: (b, qi, 0)),
        out_shape=jax.ShapeDtypeStruct((B, T, BRANCH_W), BF16),
        scratch_shapes=[pltpu.VMEM((N_HEADS, tq, 2 * LANES), BF16)] + _softmax_scratch(tq),
        compiler_params=_cparams("parallel", "parallel", "arbitrary"),
    )(qn, qr, kpp, v)


def _fox_attn_kernel(q_ref, k_ref, v_ref, cq_ref, ck_ref, o_ref, qs_ref, m_ref, l_ref, acc_ref,
                     *, tq, tk, past):
    qi = pl.program_id(1)
    ki = pl.program_id(2)
    q0 = past + qi * tq
    last_k = (q0 + tq - 1) // tk
    lane = lax.broadcasted_iota(I32, (tq, LANES), 1)

    @pl.when(ki == 0)
    def _init():
        _init_softmax(m_ref, l_ref, acc_ref)
        for h in range(N_HEADS):
            q = q_ref[0, :, (h // 2) * LANES:(h // 2 + 1) * LANES]
            qs_ref[h] = _keep_head(q, lane, h, HEAD_DIM)

    @pl.when(ki <= last_k)
    def _step():
        qpos = q0 + lax.broadcasted_iota(I32, (tq, tk), 0)
        kpos = ki * tk + lax.broadcasted_iota(I32, (tq, tk), 1)
        ok = kpos <= qpos
        for h in range(N_HEADS):
            p = h // 2
            s = _dot_nt(qs_ref[h], k_ref[0, :, p * LANES:(p + 1) * LANES])
            s = s + cq_ref[0, :, h:h + 1] - ck_ref[0, h:h + 1, :]
            s = jnp.where(ok, s, NEG)
            _softmax_step(h, s, v_ref[0, :, p * LANES:(p + 1) * LANES], m_ref, l_ref, acc_ref)

    @pl.when(ki == pl.num_programs(2) - 1)
    def _fin():
        _write_heads(o_ref, l_ref, acc_ref, lane)


def _fox_attention(q, k, v, cq, ck, past):
    B, T, _ = q.shape
    Tkp = k.shape[1]
    tq = _tile(T, 512)
    tk = _tile(Tkp, 512)

    def kblk(qi, ki):
        return jnp.minimum(ki, (past + qi * tq + tq - 1) // tk)

    kern = functools.partial(_fox_attn_kernel, tq=tq, tk=tk, past=past)
    return pl.pallas_call(
        kern, grid=(B, T // tq, Tkp // tk),
        in_specs=[pl.BlockSpec((1, tq, BRANCH_W), lambda b, qi, ki: (b, qi, 0)),
                  pl.BlockSpec((1, tk, BRANCH_W), lambda b, qi, ki: (b, kblk(qi, ki), 0)),
                  pl.BlockSpec((1, tk, BRANCH_W), lambda b, qi, ki: (b, kblk(qi, ki), 0)),
                  pl.BlockSpec((1, tq, N_HEADS), lambda b, qi, ki: (b, qi, 0)),
                  pl.BlockSpec((1, N_HEADS, tk), lambda b, qi, ki: (b, 0, kblk(qi, ki)))],
        out_specs=pl.BlockSpec((1, tq, BRANCH_W), lambda b, qi, ki: (b, qi, 0)),
        out_shape=jax.ShapeDtypeStruct((B, T, BRANCH_W), BF16),
        scratch_shapes=[pltpu.VMEM((N_HEADS, tq, LANES), BF16)] + _softmax_scratch(tq),
        compiler_params=_cparams("parallel", "parallel", "arbitrary"),
    )(q, k, v, cq, ck)


def _dsa_attn_kernel(q_ref, k_ref, v_ref, iq_ref, ikt_ref, iw_ref, o_ref,
                     key_ref, qs_ref, iqs_ref, m_ref, l_ref, acc_ref, cidx_ref,
                     *, tq, tk, past, n_keys, padded, n_keep, idx_bits):
    qi = pl.program_id(1)
    q0 = past + qi * tq
    adm_end = jnp.minimum(n_keys, ((q0 + tq - 1) // CHUNK + 1) * CHUNK)
    nkb = (adm_end + tk - 1) // tk
    lane = lax.broadcasted_iota(I32, (tq, LANES), 1)
    qpos = q0 + lax.broadcasted_iota(I32, (tq, tk), 0)
    col = lax.broadcasted_iota(I32, (tq, tk), 1)

    _init_softmax(m_ref, l_ref, acc_ref)
    for h in range(N_HEADS):
        q = q_ref[0, :, (h // 2) * LANES:(h // 2 + 1) * LANES]
        qs_ref[h] = _keep_head(q, lane, h, HEAD_DIM)
        iq = iq_ref[0, :, (h // 4) * LANES:(h // 4 + 1) * LANES]
        iqs_ref[h] = _keep_head(iq, lane, h, IDX_DIM)

    def score_body(kb, carry):
        ikt = ikt_ref[0, kb]
        sc = jnp.zeros((tq, tk), F32)
        for h in range(IDX_H):
            sc = sc + iw_ref[0, :, h:h + 1] * jnp.maximum(_dot(iqs_ref[h], ikt), 0.0)
        kpos = kb * tk + col
        bits = pltpu.bitcast(sc + 0.0, I32)
        okey = jnp.where(bits < 0, bits ^ 0x7FFFFFFF, bits)
        kchunk = jnp.right_shift(kpos, CHUNK_SHIFT)
        if padded:
            kchunk = jnp.where(kpos < n_keys, kchunk, BIG_IDX)
        adm = kchunk <= jnp.right_shift(qpos, CHUNK_SHIFT)
        key_ref[kb] = jnp.where(adm, okey, KEY_NEG_INF)
        return carry

    lax.fori_loop(0, nkb, score_body, 0)

    def count(pred):
        def body(kb, c):
            return c + jnp.sum(pred(key_ref[kb], kb), axis=1, keepdims=True)
        return lax.fori_loop(0, nkb, body, jnp.zeros((tq, 1), F32))

    def ones_where(cond):
        return jnp.where(cond, 1.0, 0.0)

    keep = float(n_keep)
    c_nonneg = count(lambda kt, kb: ones_where(kt >= 0))
    base = jnp.where(c_nonneg >= keep, jnp.zeros((tq, 1), I32), jnp.full((tq, 1), INT_MIN, I32))

    def bit_body(j, base):
        cand = base | jnp.left_shift(jnp.int32(1), 30 - j)
        c = count(lambda kt, kb: ones_where(kt >= cand))
        return jnp.where(c >= keep, cand, base)

    thr = lax.fori_loop(0, 31, bit_body, base)

    c_gt = count(lambda kt, kb: ones_where(kt > thr))
    c_ge = count(lambda kt, kb: ones_where(kt >= thr))
    need = keep - c_gt
    few = thr == KEY_NEG_INF
    tied = jnp.where(few, 0.0, ones_where(c_ge > keep))
    cidx_ref[...] = jnp.where(few, -1, BIG_IDX)

    @pl.when(jnp.max(tied) > 0.5)
    def _ties():
        def idx_body(j, b):
            cand = b | jnp.left_shift(jnp.int32(1), idx_bits - 1 - j)
            f = count(lambda kt, kb: jnp.where(kt == thr, ones_where(kb * tk + col < cand), 0.0))
            return jnp.where(f < need, cand, b)
        b = lax.fori_loop(0, idx_bits, idx_body, jnp.zeros((tq, 1), I32))
        cidx_ref[...] = jnp.where(few, -1, jnp.where(tied > 0.5, b, BIG_IDX))

    cidx = cidx_ref[...]

    def attn_body(kb, carry):
        k0 = pl.multiple_of(kb * tk, tk)
        kt = key_ref[kb]
        kpos = kb * tk + col
        bias = jnp.where(kt > thr, 0.0, jnp.where(kt == thr, jnp.where(kpos <= cidx, 0.0, NEG), NEG))
        for h in range(N_HEADS):
            p = h // 2
            s = _dot_nt(qs_ref[h], k_ref[0, pl.ds(k0, tk), p * LANES:(p + 1) * LANES]) + bias
            _softmax_step(h, s, v_ref[0, pl.ds(k0, tk), p * LANES:(p + 1) * LANES], m_ref, l_ref, acc_ref)
        return carry

    lax.fori_loop(0, nkb, attn_body, 0)
    _write_heads(o_ref, l_ref, acc_ref, lane)


def _dsa_attention(q, k, v, iq, ikt, iw, past, n_keys):
    B, T, _ = q.shape
    Tkp = k.shape[1]
    tq = _tile(T, 512)
    tk = _tile(Tkp, 512)
    nkb = Tkp // tk
    n_keep = min(TOPK_MAX, n_keys // 4)
    idx_bits = max(1, (Tkp - 1).bit_length())
    kern = functools.partial(_dsa_attn_kernel, tq=tq, tk=tk, past=past, n_keys=n_keys, padded=Tkp != n_keys,
                             n_keep=n_keep, idx_bits=idx_bits)
    return pl.pallas_call(
        kern, grid=(B, T // tq),
        in_specs=[pl.BlockSpec((1, tq, BRANCH_W), lambda b, qi: (b, qi, 0)),
                  pl.BlockSpec((1, Tkp, BRANCH_W), lambda b, qi: (b, 0, 0)),
                  pl.BlockSpec((1, Tkp, BRANCH_W), lambda b, qi: (b, 0, 0)),
                  pl.BlockSpec((1, tq, IDX_H * IDX_DIM), lambda b, qi: (b, qi, 0)),
                  pl.BlockSpec((1, nkb, LANES, tk), lambda b, qi: (b, 0, 0, 0)),
                  pl.BlockSpec((1, tq, IDX_H), lambda b, qi: (b, qi, 0))],
        out_specs=pl.BlockSpec((1, tq, BRANCH_W), lambda b, qi: (b, qi, 0)),
        out_shape=jax.ShapeDtypeStruct((B, T, BRANCH_W), BF16),
        scratch_shapes=[pltpu.VMEM((nkb, tq, tk), I32),
                        pltpu.VMEM((N_HEADS, tq, LANES), BF16),
                        pltpu.VMEM((IDX_H, tq, LANES), BF16)] + _softmax_scratch(tq)
                       + [pltpu.VMEM((tq, 1), I32)],
        compiler_params=_cparams("parallel", "arbitrary"),
    )(q, k, v, iq, ikt, iw)


def _heads_t(x_ref, qt_ref, lane, width, row0=0):
    per = LANES // width
    for h in range(N_HEADS):
        s = h // per
        x = x_ref[0, :, s * LANES:(s + 1) * LANES].astype(F32)
        x = jnp.where(_own_lanes(lane, h, width), x, 0.0)
        qt_ref[h, row0:row0 + LANES, :] = x.T.astype(BF16)


def _softmax_step_t(h, s, vt_h, m_ref, acc_ref, exp=jnp.exp):
    m_prev = m_ref[h]
    m_new = jnp.maximum(m_prev, jnp.max(s, axis=0, keepdims=True))
    alpha = exp(m_prev - m_new)
    p = exp(s - m_new)
    acc_ref[h] = alpha * acc_ref[h] + _dot(vt_h, p.astype(BF16))
    m_ref[h] = m_new


def _init_softmax_t(m_ref, acc_ref):
    m_ref[...] = jnp.full(m_ref.shape, NEG, F32)
    acc_ref[...] = jnp.zeros(acc_ref.shape, F32)


def _write_heads_t(o_ref, acc_ref):
    for p in range(N_HEADS // 2):
        halves = []
        for j, h in enumerate((2 * p, 2 * p + 1)):
            a = acc_ref[h]
            halves.append(a[j * HEAD_DIM:(j + 1) * HEAD_DIM] * (1.0 / a[LANES:LANES + 1]))
        o_ref[0, :, p * LANES:(p + 1) * LANES] = jnp.concatenate(halves, axis=0).T.astype(BF16)


def _softmax_scratch_t(tq):
    return [pltpu.VMEM((N_HEADS, 1, tq), F32), pltpu.VMEM((N_HEADS, LANES + ONES_ROWS, tq), F32)]


def _mla_attn_t_kernel(qn_ref, qr_ref, kpp_ref, vt_ref, o_ref, qt_ref, m_ref, acc_ref,
                       *, tq, tk, past, n_keys):
    qi = pl.program_id(1)
    ki = pl.program_id(2)
    q0 = past + qi * tq
    last_k = (jnp.minimum(n_keys, ((q0 + tq - 1) // CHUNK + 1) * CHUNK) - 1) // tk

    @pl.when(ki == 0)
    def _init():
        _init_softmax_t(m_ref, acc_ref)
        lane = lax.broadcasted_iota(I32, (tq, LANES), 1)
        _heads_t(qn_ref, qt_ref, lane, HEAD_DIM)
        _heads_t(qr_ref, qt_ref, lane, ROPE_A, row0=LANES)

    interior = (ki * tk + tk - 1) // CHUNK <= q0 // CHUNK

    def block(masked):
        if masked:
            kpos = ki * tk + lax.broadcasted_iota(I32, (tk, tq), 0)
            qpos = q0 + lax.broadcasted_iota(I32, (tk, tq), 1)
            ok = jnp.right_shift(kpos, CHUNK_SHIFT) <= jnp.right_shift(qpos, CHUNK_SHIFT)
        for h in range(N_HEADS):
            p = h // 2
            s = _dot(kpp_ref[0, p], qt_ref[h])
            if masked:
                s = jnp.where(ok, s, NEG)
            _softmax_step_t(h, s, vt_ref[0, 0, p], m_ref, acc_ref, exp=jnp.exp2)

    @pl.when(jnp.logical_and(ki <= last_k, interior))
    def _interior():
        block(False)

    @pl.when(jnp.logical_and(ki <= last_k, jnp.logical_not(interior)))
    def _diagonal():
        block(True)

    @pl.when(ki == pl.num_programs(2) - 1)
    def _fin():
        _write_heads_t(o_ref, acc_ref)


def _fox_attn_t_kernel(q_ref, k_ref, vt_ref, cq_ref, ck_ref, o_ref, qt_ref, m_ref, acc_ref,
                       *, tq, tk, past):
    qi = pl.program_id(1)
    ki = pl.program_id(2)
    q0 = past + qi * tq
    last_k = (q0 + tq - 1) // tk

    @pl.when(ki == 0)
    def _init():
        _init_softmax_t(m_ref, acc_ref)
        _heads_t(q_ref, qt_ref, lax.broadcasted_iota(I32, (tq, LANES), 1), HEAD_DIM)

    interior = ki * tk + tk - 1 <= q0

    def block(masked):
        if masked:
            kpos = ki * tk + lax.broadcasted_iota(I32, (tk, tq), 0)
            qpos = q0 + lax.broadcasted_iota(I32, (tk, tq), 1)
            ok = kpos <= qpos
        for h in range(N_HEADS):
            p = h // 2
            s = _dot(k_ref[0, :, p * LANES:(p + 1) * LANES], qt_ref[h])
            s = s + cq_ref[0, h:h + 1, :] - ck_ref[0, :, h:h + 1]
            if masked:
                s = jnp.where(ok, s, NEG)
            _softmax_step_t(h, s, vt_ref[0, 0, p], m_ref, acc_ref)

    @pl.when(jnp.logical_and(ki <= last_k, interior))
    def _interior():
        block(False)

    @pl.when(jnp.logical_and(ki <= last_k, jnp.logical_not(interior)))
    def _diagonal():
        block(True)

    @pl.when(ki == pl.num_programs(2) - 1)
    def _fin():
        _write_heads_t(o_ref, acc_ref)


def _blocked_t(v, tk):
    B, Tk, _ = v.shape
    vt = jnp.swapaxes(v.reshape(B, Tk // tk, tk, N_HEADS // 2, LANES), 2, 3)
    vt = jnp.swapaxes(vt, 3, 4)
    return jnp.concatenate([vt, jnp.ones(vt.shape[:3] + (ONES_ROWS, tk), vt.dtype)], axis=3)


def _mla_attention_t(qn, qr, kpp, v, past, n_keys):
    B, T, _ = qn.shape
    Tk = v.shape[1]
    tq = _tile(T, 512)
    tk = _tile(Tk, 512)

    def kblk(qi, ki):
        last = (jnp.minimum(n_keys, ((past + qi * tq + tq - 1) // CHUNK + 1) * CHUNK) - 1) // tk
        return jnp.minimum(ki, last)

    kern = functools.partial(_mla_attn_t_kernel, tq=tq, tk=tk, past=past, n_keys=n_keys)
    return pl.pallas_call(
        kern, grid=(B, T // tq, Tk // tk),
        in_specs=[pl.BlockSpec((1, tq, BRANCH_W), lambda b, qi, ki: (b, qi, 0)),
                  pl.BlockSpec((1, tq, 2 * LANES), lambda b, qi, ki: (b, qi, 0)),
                  pl.BlockSpec((1, N_HEADS // 2, tk, 2 * LANES), lambda b, qi, ki: (b, 0, kblk(qi, ki), 0)),
                  pl.BlockSpec((1, 1, N_HEADS // 2, LANES + ONES_ROWS, tk), lambda b, qi, ki: (b, kblk(qi, ki), 0, 0, 0))],
        out_specs=pl.BlockSpec((1, tq, BRANCH_W), lambda b, qi, ki: (b, qi, 0)),
        out_shape=jax.ShapeDtypeStruct((B, T, BRANCH_W), BF16),
        scratch_shapes=[pltpu.VMEM((N_HEADS, 2 * LANES, tq), BF16)] + _softmax_scratch_t(tq),
        compiler_params=_cparams("parallel", "parallel", "arbitrary"),
    )(qn, qr, kpp, _blocked_t(v, tk))


def _fox_attention_t(q, k, v, cq_t, ck, past):
    B, T, _ = q.shape
    Tk = k.shape[1]
    tq = _tile(T, 512)
    tk = _tile(Tk, 512)

    def kblk(qi, ki):
        return jnp.minimum(ki, (past + qi * tq + tq - 1) // tk)

    kern = functools.partial(_fox_attn_t_kernel, tq=tq, tk=tk, past=past)
    return pl.pallas_call(
        kern, grid=(B, T // tq, Tk // tk),
        in_specs=[pl.BlockSpec((1, tq, BRANCH_W), lambda b, qi, ki: (b, qi, 0)),
                  pl.BlockSpec((1, tk, BRANCH_W), lambda b, qi, ki: (b, kblk(qi, ki), 0)),
                  pl.BlockSpec((1, 1, N_HEADS // 2, LANES + ONES_ROWS, tk), lambda b, qi, ki: (b, kblk(qi, ki), 0, 0, 0)),
                  pl.BlockSpec((1, N_HEADS, tq), lambda b, qi, ki: (b, 0, qi)),
                  pl.BlockSpec((1, tk, N_HEADS), lambda b, qi, ki: (b, kblk(qi, ki), 0))],
        out_specs=pl.BlockSpec((1, tq, BRANCH_W), lambda b, qi, ki: (b, qi, 0)),
        out_shape=jax.ShapeDtypeStruct((B, T, BRANCH_W), BF16),
        scratch_shapes=[pltpu.VMEM((N_HEADS, LANES, tq), BF16)] + _softmax_scratch_t(tq),
        compiler_params=_cparams("parallel", "parallel", "arbitrary"),
    )(q, k, _blocked_t(v, tk), cq_t, ck)


def _dsa_attn_t_kernel(q_ref, k_ref, vt_ref, iq_ref, ik_ref, iw_ref, o_ref,
                       key_ref, qt_ref, iqt_ref, m_ref, acc_ref, cidx_ref,
                       *, tq, tk, past, n_keys, n_keep, idx_bits):
    qi = pl.program_id(1)
    q0 = past + qi * tq
    adm_end = jnp.minimum(n_keys, ((q0 + tq - 1) // CHUNK + 1) * CHUNK)
    nkb = (adm_end + tk - 1) // tk
    lane = lax.broadcasted_iota(I32, (tq, LANES), 1)
    qchunk = jnp.right_shift(q0 + lax.broadcasted_iota(I32, (tk, tq), 1), CHUNK_SHIFT)
    krow = lax.broadcasted_iota(I32, (tk, tq), 0)

    _init_softmax_t(m_ref, acc_ref)
    _heads_t(q_ref, qt_ref, lane, HEAD_DIM)
    _heads_t(iq_ref, iqt_ref, lane, IDX_DIM)

    def score_body(kb, carry):
        k0 = pl.multiple_of(kb * tk, tk)
        ik = ik_ref[0, pl.ds(k0, tk), :]
        sc = jnp.zeros((tk, tq), F32)
        for h in range(IDX_H):
            sc = sc + iw_ref[0, h:h + 1, :] * jnp.maximum(_dot(ik, iqt_ref[h]), 0.0)
        bits = pltpu.bitcast(sc + 0.0, I32)
        okey = jnp.where(bits < 0, bits ^ 0x7FFFFFFF, bits)
        adm = jnp.right_shift(k0 + krow, CHUNK_SHIFT) <= qchunk
        key_ref[kb] = jnp.where(adm, okey, KEY_NEG_INF)
        return carry

    lax.fori_loop(0, nkb, score_body, 0)

    def count(pred):
        def body(kb, c):
            return c + jnp.sum(pred(key_ref[kb], kb), axis=0, keepdims=True)
        return lax.fori_loop(0, nkb, body, jnp.zeros((1, tq), F32))

    def ones_where(cond):
        return jnp.where(cond, 1.0, 0.0)

    keep = float(n_keep)
    c_nonneg = count(lambda kt, kb: ones_where(kt >= 0))
    base = jnp.where(c_nonneg >= keep, jnp.zeros((1, tq), I32), jnp.full((1, tq), INT_MIN, I32))

    def bit_body(j, base):
        cand = base | jnp.left_shift(jnp.int32(1), 30 - j)
        c = count(lambda kt, kb: ones_where(kt >= cand))
        return jnp.where(c >= keep, cand, base)

    thr = lax.fori_loop(0, 31, bit_body, base)

    c_gt = count(lambda kt, kb: ones_where(kt > thr))
    c_ge = count(lambda kt, kb: ones_where(kt >= thr))
    need = keep - c_gt
    few = thr == KEY_NEG_INF
    tied = jnp.where(few, 0.0, ones_where(c_ge > keep))
    cidx_ref[...] = jnp.where(few, -1, BIG_IDX)

    @pl.when(jnp.max(tied) > 0.5)
    def _ties():
        def idx_body(j, b):
            cand = b | jnp.left_shift(jnp.int32(1), idx_bits - 1 - j)
            f = count(lambda kt, kb: jnp.where(kt == thr, ones_where(kb * tk + krow < cand), 0.0))
            return jnp.where(f < need, cand, b)
        b = lax.fori_loop(0, idx_bits, idx_body, jnp.zeros((1, tq), I32))
        cidx_ref[...] = jnp.where(few, -1, jnp.where(tied > 0.5, b, BIG_IDX))

    cidx = cidx_ref[...]

    def attn_body(kb, carry):
        k0 = pl.multiple_of(kb * tk, tk)
        kt = key_ref[kb]
        kpos = k0 + krow
        bias = jnp.where(kt > thr, 0.0, jnp.where(kt == thr, jnp.where(kpos <= cidx, 0.0, NEG), NEG))
        for h in range(N_HEADS):
            p = h // 2
            s = _dot(k_ref[0, pl.ds(k0, tk), p * LANES:(p + 1) * LANES], qt_ref[h]) + bias
            _softmax_step_t(h, s, vt_ref[0, kb, p], m_ref, acc_ref)
        return carry

    lax.fori_loop(0, nkb, attn_body, 0)
    _write_heads_t(o_ref, acc_ref)


def _dsa_attention_t(q, k, v, iq, ik4, iw_t, past, n_keys):
    B, T, _ = q.shape
    Tk = k.shape[1]
    tq = _tile(T, 512)
    tk = _tile(Tk, 512)
    nkb = Tk // tk
    n_keep = min(TOPK_MAX, n_keys // 4)
    idx_bits = max(1, (Tk - 1).bit_length())
    kern = functools.partial(_dsa_attn_t_kernel, tq=tq, tk=tk, past=past, n_keys=n_keys,
                             n_keep=n_keep, idx_bits=idx_bits)
    return pl.pallas_call(
        kern, grid=(B, T // tq),
        in_specs=[pl.BlockSpec((1, tq, BRANCH_W), lambda b, qi: (b, qi, 0)),
                  pl.BlockSpec((1, Tk, BRANCH_W), lambda b, qi: (b, 0, 0)),
                  pl.BlockSpec((1, nkb, N_HEADS // 2, LANES + ONES_ROWS, tk), lambda b, qi: (b, 0, 0, 0, 0)),
                  pl.BlockSpec((1, tq, IDX_H * IDX_DIM), lambda b, qi: (b, qi, 0)),
                  pl.BlockSpec((1, Tk, LANES), lambda b, qi: (b, 0, 0)),
                  pl.BlockSpec((1, IDX_H, tq), lambda b, qi: (b, 0, qi))],
        out_specs=pl.BlockSpec((1, tq, BRANCH_W), lambda b, qi: (b, qi, 0)),
        out_shape=jax.ShapeDtypeStruct((B, T, BRANCH_W), BF16),
        scratch_shapes=[pltpu.VMEM((nkb, tk, tq), I32),
                        pltpu.VMEM((N_HEADS, LANES, tq), BF16),
                        pltpu.VMEM((IDX_H, LANES, tq), BF16)] + _softmax_scratch_t(tq)
                       + [pltpu.VMEM((1, tq), I32)],
        compiler_params=_cparams("parallel", "arbitrary"),
    )(q, k, _blocked_t(v, tk), iq, ik4, iw_t)


def _merge_kernel(h_ref, hb_ref, oa_ref, ob_ref, oc_ref, wg_ref, bg_ref, wbr_ref, wout_ref, g_ref, b_ref,
                  wr_ref, br_ref, x_ref, xb_ref, idx_ref, gate_ref):
    hb = hb_ref[...]
    d = D_MODEL
    merged = None
    for n, o_ref in enumerate((oa_ref, ob_ref, oc_ref)):
        gt = _dot(hb, wg_ref[:, n * d:(n + 1) * d]) + bg_ref[:, n * d:(n + 1) * d]
        term = (1.0 / (1.0 + jnp.exp(-gt))) * _dot(o_ref[...], wbr_ref[n])
        merged = term if merged is None else merged + term
    out = _dot(merged.astype(BF16), wout_ref[...])
    x = _layer_norm(DN_ALPHA * h_ref[...] + out, g_ref[...], b_ref[...])
    xb = x.astype(BF16)
    x_ref[...] = x
    xb_ref[...] = xb

    logits = _dot(xb, wr_ref[...]) + br_ref[...]
    tm = logits.shape[0]
    lane = lax.broadcasted_iota(I32, (tm, N_EXPERTS), 1).astype(F32)
    k4 = lax.broadcasted_iota(I32, (tm, TOP_K), 1)
    vals = jnp.zeros((tm, TOP_K), F32)
    idxs = jnp.zeros((tm, TOP_K), F32)
    for k in range(TOP_K):
        mx = jnp.max(logits, axis=1, keepdims=True)
        ix = jnp.min(jnp.where(logits == mx, lane, float(N_EXPERTS)), axis=1, keepdims=True)
        vals = jnp.where(k4 == k, mx, vals)
        idxs = jnp.where(k4 == k, ix, idxs)
        logits = jnp.where(lane == ix, -jnp.inf, logits)
    e = jnp.exp(vals - jnp.max(vals, axis=1, keepdims=True))
    idx_ref[...] = idxs.astype(I32)
    gate_ref[...] = e * (1.0 / jnp.sum(e, axis=1, keepdims=True))


def _merge(h, hb, oa, ob, oc, wg, bg, wbr, wout, g, b, wr, br):
    n, d = h.shape
    tm = _tile(n, 512)
    row = lambda w: pl.BlockSpec((tm, w), lambda i: (i, 0))
    return pl.pallas_call(
        _merge_kernel, grid=(n // tm,),
        in_specs=[row(d), row(d), row(BRANCH_W), row(BRANCH_W), row(BRANCH_W),
                  _full(wg.shape), _full(bg.shape), _full(wbr.shape), _full(wout.shape),
                  _full(g.shape), _full(b.shape), _full(wr.shape), _full(br.shape)],
        out_specs=[row(d), row(d), row(TOP_K), row(TOP_K)],
        out_shape=[jax.ShapeDtypeStruct((n, d), F32), jax.ShapeDtypeStruct((n, d), BF16),
                   jax.ShapeDtypeStruct((n, TOP_K), I32), jax.ShapeDtypeStruct((n, TOP_K), F32)],
        compiler_params=_cparams("parallel"),
    )(h, hb, oa, ob, oc, wg, bg, wbr, wout, g, b, wr, br)


def _expert_kernel(blk_e_ref, idx_hbm, x_hbm, w1_ref, b1_ref, w2_ref, b2_ref, y_hbm,
                   idx_smem, xbuf, ybuf, isem, gsem, ssem, *, bs, n_blocks):
    del blk_e_ref
    i = pl.program_id(0)

    def idx_copy(row, s):
        return pltpu.make_async_copy(idx_hbm.at[row], idx_smem.at[s], isem.at[s])

    def start_gather(s):
        for r in range(bs):
            tok = idx_smem[s, r]
            pltpu.make_async_copy(x_hbm.at[pl.ds(tok, 1)], xbuf.at[s, pl.ds(r, 1)], gsem.at[s]).start()

    def wait_gather(s):
        pltpu.make_async_copy(x_hbm.at[pl.ds(0, bs)], xbuf.at[s], gsem.at[s]).wait()

    def start_scatter(s_buf, s_idx):
        for r in range(bs):
            dst = idx_smem[s_idx, bs + r]
            pltpu.make_async_copy(ybuf.at[s_buf, pl.ds(r, 1)], y_hbm.at[pl.ds(dst, 1)], ssem.at[s_buf]).start()

    def wait_scatter(s):
        pltpu.make_async_copy(ybuf.at[s], y_hbm.at[pl.ds(0, bs)], ssem.at[s]).wait()

    def step(slot, gather_next, scatter_prev):
        nslot = 1 - slot
        idx_copy(i + 2, slot).start()
        idx_copy(i + 1, nslot).wait()
        wait_gather(slot)

        @pl.when(i >= 2)
        def _():
            wait_scatter(slot)

        if gather_next:
            start_gather(nslot)
        if scatter_prev:
            start_scatter(nslot, nslot)
        x = xbuf[slot].astype(BF16)
        gu = _dot(x, w1_ref[0]) + b1_ref[0]
        g = jnp.minimum(gu[:, :D_FF], SWIGLU_LIMIT)
        u = jnp.clip(gu[:, D_FF:], -SWIGLU_LIMIT, SWIGLU_LIMIT)
        a = g * (1.0 / (1.0 + jnp.exp(-SWIGLU_ALPHA * g))) * (u + 1.0)
        ybuf[slot] = _dot(a.astype(BF16), w2_ref[0]) + b2_ref[0]

    last = n_blocks - 1
    middle = jnp.logical_and(i > 0, i < last)

    @pl.when(i == 0)
    def _first():
        idx_copy(0, 0).start()
        idx_copy(0, 0).wait()
        start_gather(0)
        idx_copy(1, 1).start()
        step(0, True, False)

    @pl.when(jnp.logical_and(middle, i % 2 == 0))
    def _middle_even():
        step(0, True, True)

    @pl.when(jnp.logical_and(middle, i % 2 == 1))
    def _middle_odd():
        step(1, True, True)

    @pl.when(i == last)
    def _last():
        slot = last % 2
        step(slot, False, True)
        idx_copy(i + 2, slot).wait()
        start_scatter(slot, slot)
        wait_scatter(1 - slot)
        wait_scatter(slot)


def _experts(x, blk_e, idx, w1, b1, w2, b2, n_rows_out, bs):
    n, d = x.shape
    n_blocks = idx.shape[0] - 2
    assert n_blocks >= 2
    kern = functools.partial(_expert_kernel, bs=bs, n_blocks=n_blocks)
    grid_spec = pltpu.PrefetchScalarGridSpec(
        num_scalar_prefetch=1, grid=(n_blocks,),
        in_specs=[pl.BlockSpec(memory_space=pl.ANY), pl.BlockSpec(memory_space=pl.ANY),
                  pl.BlockSpec((1, d, 2 * D_FF), lambda i, be: (be[i], 0, 0)),
                  pl.BlockSpec((1, 1, 2 * D_FF), lambda i, be: (be[i], 0, 0)),
                  pl.BlockSpec((1, D_FF, d), lambda i, be: (be[i], 0, 0)),
                  pl.BlockSpec((1, 1, d), lambda i, be: (be[i], 0, 0))],
        out_specs=pl.BlockSpec(memory_space=pl.ANY),
        scratch_shapes=[pltpu.SMEM((2, 2 * bs), I32),
                        pltpu.VMEM((2, bs, d), F32), pltpu.VMEM((2, bs, d), F32),
                        pltpu.SemaphoreType.DMA((2,)), pltpu.SemaphoreType.DMA((2,)),
                        pltpu.SemaphoreType.DMA((2,))])
    return pl.pallas_call(
        kern, grid_spec=grid_spec,
        out_shape=jax.ShapeDtypeStruct((n_rows_out, d), F32),
        compiler_params=_cparams("arbitrary"),
    )(blk_e, idx, x, w1, b1, w2, b2)


def _dispatch_tables(top_idx, bs):
    n = top_idx.shape[0]
    m = n * TOP_K
    flat_e = top_idx.reshape(-1)
    order = jnp.argsort(flat_e).astype(I32)
    counts = jnp.bincount(flat_e, length=N_EXPERTS).astype(I32)
    padded = (counts + bs - 1) // bs * bs
    start = jnp.cumsum(counts) - counts
    pend = jnp.cumsum(padded)
    pstart = pend - padded
    n_blocks = -(-m // bs) + N_EXPERTS
    cap = n_blocks * bs
    blk_start = jnp.arange(n_blocks, dtype=I32) * bs
    blk_e = jnp.minimum(jnp.sum((pend[None, :] <= blk_start[:, None]).astype(I32), axis=1), N_EXPERTS - 1)
    slot = jnp.arange(cap, dtype=I32)
    per_slot = lambda per_blk: jnp.broadcast_to(per_blk[:, None], (n_blocks, bs)).reshape(cap)
    e = per_slot(blk_e)
    is_pad = slot - per_slot(pstart[blk_e]) >= per_slot(counts[blk_e])
    ext = jnp.concatenate([jnp.zeros((cap,), I32), order, jnp.zeros((cap,), I32)])
    shift = pstart - start

    def place(ex, acc):
        return jnp.where(e == ex, lax.dynamic_slice(ext, (cap - shift[ex],), (cap,)), acc)

    flat = lax.fori_loop(0, N_EXPERTS, place, jnp.zeros((cap,), I32))
    src = jnp.where(is_pad, 0, flat // TOP_K)
    pads_before = slot - per_slot((start + counts)[blk_e])
    dst = jnp.where(is_pad, m + pads_before, (flat % TOP_K) * n + flat // TOP_K)
    zeros = jnp.zeros((2, bs), I32)
    idx = jnp.concatenate([jnp.concatenate([src.reshape(n_blocks, bs), zeros], axis=0),
                           jnp.concatenate([zeros, dst.reshape(n_blocks, bs)], axis=0)], axis=1)
    return blk_e, idx, cap


def _combine_kernel(x_ref, y0_ref, y1_ref, y2_ref, y3_ref, gate_ref, g_ref, b_ref, o_ref, ob_ref):
    y = gate_ref[:, 0:1] * y0_ref[...]
    for k, y_ref in enumerate((y1_ref, y2_ref, y3_ref), start=1):
        y = y + gate_ref[:, k:k + 1] * y_ref[...]
    out = _layer_norm(DN_ALPHA * x_ref[...] + y, g_ref[...], b_ref[...])
    o_ref[...] = out
    ob_ref[...] = out.astype(BF16)


def _combine(x, y, gate, g, b):
    n, d = x.shape
    tm = _tile(n, 256)
    nt = n // tm
    row = lambda w: pl.BlockSpec((tm, w), lambda i: (i, 0))
    y_specs = [pl.BlockSpec((tm, d), lambda i, k=k: (k * nt + i, 0)) for k in range(TOP_K)]
    return pl.pallas_call(
        _combine_kernel, grid=(nt,),
        in_specs=[row(d)] + y_specs + [row(TOP_K), _full((1, d)), _full((1, d))],
        out_specs=[row(d), row(d)],
        out_shape=[jax.ShapeDtypeStruct((n, d), F32), jax.ShapeDtypeStruct((n, d), BF16)],
        compiler_params=_cparams("parallel"),
    )(x, y, y, y, y, gate, g, b)


def _deinterleave_kernel(w_ref, sel_ref, o_ref):
    half = o_ref.shape[2] // 2
    for s in range(half // LANES):
        y = _dot(w_ref[0, :, 2 * s * LANES:2 * (s + 1) * LANES].astype(BF16), sel_ref[...])
        o_ref[0, :, s * LANES:(s + 1) * LANES] = y[:, :LANES].astype(BF16)
        o_ref[0, :, half + s * LANES:half + (s + 1) * LANES] = y[:, LANES:].astype(BF16)


def _cast_kernel(w_ref, o_ref):
    o_ref[...] = w_ref[...].astype(BF16)


def _cast_down(w_down):
    L, E, f, d = w_down.shape
    blk = pl.BlockSpec((1, f, d), lambda i: (i, 0, 0))
    out = pl.pallas_call(
        _cast_kernel, grid=(L * E,), in_specs=[blk], out_specs=blk,
        out_shape=jax.ShapeDtypeStruct((L * E, f, d), BF16),
        compiler_params=_cparams("parallel"),
    )(w_down.reshape(L * E, f, d))
    return out.reshape(L, E, f, d)


def _deinterleave_up(w_up):
    L, E, d, f2 = w_up.shape
    j = jnp.arange(2 * LANES)
    sel = (j[:, None] == jnp.where(j < LANES, 2 * j, 2 * (j - LANES) + 1)[None, :]).astype(BF16)
    out = pl.pallas_call(
        _deinterleave_kernel, grid=(L * E,),
        in_specs=[pl.BlockSpec((1, d, f2), lambda i: (i, 0, 0)), _full((2 * LANES, 2 * LANES))],
        out_specs=pl.BlockSpec((1, d, f2), lambda i: (i, 0, 0)),
        out_shape=jax.ShapeDtypeStruct((L * E, d, f2), BF16),
        compiler_params=_cparams("parallel"),
    )(w_up.reshape(L * E, d, f2), sel)
    return out.reshape(L, E, d, f2)


def _prep_layer(l, w_in, b_f, b_gate, g_qa, g_kva, w_uq, w_ukv, w_br, w_out, ln1_g, ln1_b,
                w_router, b_router, w_up, b_up, w_down, b_down, ln2_g, ln2_b):
    w = w_in[l]
    o = [0]

    def take(width):
        s = w[:, o[0]:o[0] + width]
        o[0] += width
        return s

    def pad_to(a, width):
        return jnp.pad(a, ((0, 0), (0, width - a.shape[1])))

    qa, kva, kra = take(Q_LORA), take(KV_LORA), take(ROPE_A)
    qb, kb, vb, fb = take(BRANCH_W), take(BRANCH_W), take(BRANCH_W), take(N_HEADS)
    qc, kc, vc = take(BRANCH_W), take(BRANCH_W), take(BRANCH_W)
    iqc, ikc, iwc = take(IDX_H * IDX_DIM), take(IDX_DIM), take(IDX_H)
    gt = take(N_BRANCH * D_MODEL)
    p = {}
    p['w_mla'] = jnp.concatenate([qa, kva, jnp.tile(kra, (1, 4))], axis=1).astype(BF16)
    p['w_fox'] = jnp.concatenate([qb, kb, vb, pad_to(fb, LANES)], axis=1).astype(BF16)
    p['w_dsa'] = jnp.concatenate([qc, kc, vc, iqc, jnp.tile(ikc, (1, 4)), pad_to(iwc, LANES)], axis=1).astype(BF16)
    p['w_gate'] = gt.astype(BF16)
    uq = w_uq[l].reshape(Q_LORA, N_HEADS, HEAD_DIM + ROPE_A)
    p['w_uq'] = jnp.concatenate([uq[:, :, :HEAD_DIM].reshape(Q_LORA, -1),
                                 uq[:, :, HEAD_DIM:].reshape(Q_LORA, -1)], axis=1).astype(BF16)
    ukv = w_ukv[l].reshape(KV_LORA, N_HEADS, 2 * HEAD_DIM)
    p['w_ukv'] = jnp.concatenate([ukv[:, :, :HEAD_DIM].reshape(KV_LORA, -1),
                                  ukv[:, :, HEAD_DIM:].reshape(KV_LORA, -1)], axis=1).astype(BF16)
    p['b_f'] = b_f[l].reshape(1, N_HEADS)
    p['b_gate'] = b_gate[l].reshape(1, N_BRANCH * D_MODEL)
    p['g_qa'] = g_qa[l].reshape(1, Q_LORA)
    p['g_kva'] = g_kva[l].reshape(1, KV_LORA)
    p['w_br'] = w_br[l].astype(BF16)
    p['w_out'] = w_out[l].astype(BF16)
    p['ln1'] = (ln1_g[l].reshape(1, -1), ln1_b[l].reshape(1, -1))
    p['ln2'] = (ln2_g[l].reshape(1, -1), ln2_b[l].reshape(1, -1))
    p['w_router'] = w_router[l].astype(BF16)
    p['b_router'] = b_router[l].reshape(1, N_EXPERTS)
    p['w_up'] = w_up[l]
    p['b_up'] = jnp.concatenate([b_up[l][:, 0::2], b_up[l][:, 1::2]], axis=1)[:, None, :]
    p['w_down'] = w_down[l]
    p['b_down'] = b_down[l][:, None, :]
    return p


def _pad_time(a, t_pad):
    return a if a.shape[1] == t_pad else jnp.pad(a, ((0, 0), (0, t_pad - a.shape[1])) + ((0, 0),) * (a.ndim - 2))


def _trunk(x, cache, params, ln_in, moe_block):
    B, T, D = x.shape
    n = B * T
    past = 0 if cache is None else cache[0].shape[2]
    n_keys = past + T
    tkp = -(-n_keys // LANES) * LANES
    pos = jnp.arange(past, past + T, dtype=I32)
    pb, pt = (B, T) if T % LANES == 0 else (1, n)
    tables = lambda rot, period: tuple(jnp.tile(t, (n // (pb * T), 1)) for t in _rope_tables(pos, rot, period))
    tab_a = tables(ROPE_A, ROPE_A)
    tab_c = tables(ROT_C, HEAD_DIM)
    tab_i = tables(ROT_IDX, IDX_DIM)
    tile_mat = (jnp.arange(ROPE_A)[:, None] == (jnp.arange(LANES)[None, :] % ROPE_A)).astype(BF16)
    tm = _tile(pt, 512)
    key_major = T % LANES == 0 and tkp == n_keys

    def project(kernel, hb, weights, tabs, outs):
        return [o.reshape(B, T, -1) for o in _proj_call(kernel, hb.reshape(pb, pt, D), weights, tabs, outs, tm)]

    h, hb = _ln_in(x.reshape(n, D), *ln_in)
    rows = [[] for _ in range(8)]
    for l, p in enumerate(params):
        qn, qr, ckv_new, kr_new = project(
            _mla_proj_kernel, hb, [p['w_mla'], p['w_uq'], p['g_qa'], p['g_kva']], tab_a,
            [(BRANCH_W, BF16), (2 * LANES, BF16), (KV_LORA, F32), (ROPE_A, F32)])
        qb, kb_new, vb_new, kb16, vb16, lf_new = project(
            _fox_proj_kernel, hb, [p['w_fox'], p['b_f']], (),
            [(BRANCH_W, BF16), (BRANCH_W, F32), (BRANCH_W, F32), (BRANCH_W, BF16), (BRANCH_W, BF16),
             (N_HEADS, F32)])
        qc, kc_new, vc_new, kc16, vc16, iq, ik_new, iw = project(
            _dsa_proj_kernel, hb, [p['w_dsa']], tab_c + tab_i,
            [(BRANCH_W, BF16), (BRANCH_W, F32), (BRANCH_W, F32), (BRANCH_W, BF16), (BRANCH_W, BF16),
             (IDX_H * IDX_DIM, BF16), (IDX_DIM, F32), (IDX_H, F32)])

        if cache is None:
            ckv_all, kr_all, lf_all, ik_all = ckv_new, kr_new, lf_new, ik_new
            kb_all, vb_all, kc_all, vc_all = kb16, vb16, kc16, vc16
        else:
            c_ckv, c_kr, c_kb, c_vb, c_lf, c_kc, c_vc, c_ik = (c[l] for c in cache)
            flat = lambda c: c.reshape(B, past, BRANCH_W).astype(BF16)
            ckv_all = jnp.concatenate([c_ckv, ckv_new], axis=1)
            kr_all = jnp.concatenate([c_kr, kr_new], axis=1)
            lf_all = jnp.concatenate([c_lf, lf_new], axis=1)
            ik_all = jnp.concatenate([c_ik, ik_new], axis=1)
            kb_all = jnp.concatenate([flat(c_kb), kb16], axis=1)
            vb_all = jnp.concatenate([flat(c_vb), vb16], axis=1)
            kc_all = jnp.concatenate([flat(c_kc), kc16], axis=1)
            vc_all = jnp.concatenate([flat(c_vc), vc16], axis=1)
        ckv_all, kr_all, lf_all, ik_all, kb_all, vb_all, kc_all, vc_all = (
            _pad_time(a, tkp) for a in (ckv_all, kr_all, lf_all, ik_all, kb_all, vb_all, kc_all, vc_all))

        kpp, v_a = _mla_kv(ckv_all, kr_all, p['w_ukv'], tile_mat)
        cum = _cumsum_time(jnp.swapaxes(lf_all, 1, 2))
        if key_major:
            o_a = _mla_attention_t(qn, qr, kpp, v_a, past, n_keys)
            o_b = _fox_attention_t(qb, kb_all, vb_all, cum[:, :, past:past + T], jnp.swapaxes(cum, 1, 2), past)
            ik4 = jnp.tile(ik_all.astype(BF16), (1, 1, LANES // IDX_DIM))
            o_c = _dsa_attention_t(qc, kc_all, vc_all, iq, ik4, jnp.swapaxes(iw, 1, 2), past, n_keys)
        else:
            o_a = _mla_attention(qn, qr, kpp, v_a, past, n_keys)
            cq = jnp.swapaxes(cum[:, :, past:past + T], 1, 2)
            o_b = _fox_attention(qb, kb_all, vb_all, cq, cum, past)
            tk = _tile(tkp, 512)
            ikt = jnp.swapaxes(ik_all.astype(BF16), 1, 2)
            ikt = jnp.tile(ikt, (1, LANES // IDX_DIM, 1))
            ikt = jnp.swapaxes(ikt.reshape(B, LANES, tkp // tk, tk), 1, 2)
            o_c = _dsa_attention(qc, kc_all, vc_all, iq, ikt, iw, past, n_keys)

        x1, x1b, top_idx, gate = _merge(
            h, hb, o_a.reshape(n, -1), o_b.reshape(n, -1), o_c.reshape(n, -1),
            p['w_gate'], p['b_gate'], p['w_br'], p['w_out'], *p['ln1'], p['w_router'], p['b_router'])

        blk_e, idx, n_rows = _dispatch_tables(top_idx, moe_block)
        y4 = _experts(x1, blk_e, idx, p['w_up'], p['b_up'], p['w_down'], p['b_down'], n_rows, moe_block)
        h, hb = _combine(x1, y4, gate, *p['ln2'])

        new = (ckv_new, kr_new, kb_new.reshape(B, T, N_HEADS, HEAD_DIM), vb_new.reshape(B, T, N_HEADS, HEAD_DIM),
               lf_new, kc_new.reshape(B, T, N_HEADS, HEAD_DIM), vc_new.reshape(B, T, N_HEADS, HEAD_DIM), ik_new)
        for acc, r in zip(rows, new):
            acc.append(r)
    return h.reshape(B, T, D), [jnp.stack(a) for a in rows]


def kernel(x_prompt, x_sample, cache_mla_ckv, cache_mla_krope, cache_fox_k, cache_fox_v, cache_fox_logf,
           cache_dsa_k, cache_dsa_v, cache_dsa_idxk, ln_in_g, ln_in_b, w_in, b_f, b_gate, g_qa, g_kva,
           w_uq, w_ukv, w_br, w_out, ln1_g, ln1_b, w_router, b_router, w_up, b_up, w_down, b_down,
           ln2_g, ln2_b):
    depth = w_in.shape[0]
    w_up = _deinterleave_up(w_up)
    w_down = _cast_down(w_down)
    params = [_prep_layer(l, w_in, b_f, b_gate, g_qa, g_kva, w_uq, w_ukv, w_br, w_out, ln1_g, ln1_b,
                          w_router, b_router, w_up, b_up, w_down, b_down, ln2_g, ln2_b) for l in range(depth)]
    ln_in = (ln_in_g, ln_in_b)
    y_p, st_p = _trunk(x_prompt, None, params, ln_in, moe_block=512)
    caches = (cache_mla_ckv, cache_mla_krope, cache_fox_k, cache_fox_v, cache_fox_logf,
              cache_dsa_k, cache_dsa_v, cache_dsa_idxk)
    y_s, st_s = _trunk(x_sample, caches, params, ln_in, moe_block=128)
    out = [y_p, y_s]
    for a, b in zip(st_p, st_s):
        out += [a, b]
    return tuple(out)
```
